```python
import jax, jax.numpy as jnp
from jax import lax
import numpy as np

D_MODEL = 1024
BATCH = 2
SEQ = 8192
DEPTH = 4

GRID_W = 64
D_MIX = 2 * D_MODEL
POOL_WIDTH = D_MIX // 4
LRU_WIDTH = 3 * D_MIX // 8
ATTN_WIDTH = 3 * D_MIX // 8
POOL_WINDOWS = (2, 4, 8, 16)
POOL_GROUPS = len(POOL_WINDOWS)
POOL_GROUP_DIM = POOL_WIDTH // POOL_GROUPS
LRU_BLOCKS = 6
LRU_BLOCK_DIM = LRU_WIDTH // LRU_BLOCKS
LRU_C = 8.0
CONV_WIDTH = 4
CONV_PAD_LEFT = 1
HEAD_DIM = 128
N_Q_HEADS = ATTN_WIDTH // HEAD_DIM
N_KV_HEADS = 2
GQA_GROUP = N_Q_HEADS // N_KV_HEADS
KV_WIDTH = N_KV_HEADS * HEAD_DIM
ROPE_AXIS_DIM = HEAD_DIM // 2
ROPE_BASE = 10000.0
Q_BLOCK = 128
EPS = 1e-6
IN_SIZES = (POOL_WIDTH, POOL_WIDTH, LRU_WIDTH, LRU_WIDTH, ATTN_WIDTH, KV_WIDTH, KV_WIDTH, ATTN_WIDTH)
D_IN = sum(IN_SIZES)

kernel_name = "hymba_pool_rglru_axialgqa_encoder"


def rmsnorm(x, g):
    xf = x.astype(jnp.float32)
    y = xf * lax.rsqrt(jnp.mean(xf * xf, axis=-1, keepdims=True) + EPS)
    return (y * g.astype(jnp.float32)).astype(x.dtype)


def pool_mixer(u, w_pool, scale):
    B, L, _ = u.shape
    uf = u.astype(jnp.float32)
    cs = jnp.concatenate([jnp.zeros((B, 1, POOL_WIDTH), jnp.float32), jnp.cumsum(uf, axis=1)], axis=1)
    t = jnp.arange(L)
    means = []
    for g, w in enumerate(POOL_WINDOWS):
        half = w // 2
        lo = jnp.clip(t - half, 0, L)
        hi = jnp.clip(t + half, 0, L)
        csg = cs[..., g * POOL_GROUP_DIM:(g + 1) * POOL_GROUP_DIM]
        cnt = (hi - lo).astype(jnp.float32)[None, :, None]
        means.append((csg[:, hi] - csg[:, lo]) / cnt)
    pooled = jnp.stack(means, axis=2) - uf.reshape(B, L, POOL_GROUPS, POOL_GROUP_DIM)
    mixed = jnp.einsum('blgc,gcd->blgd', pooled.astype(u.dtype), w_pool)
    return mixed.reshape(B, L, POOL_WIDTH) * scale


def short_conv(u, w, b):
    L = u.shape[1]
    up = jnp.pad(u, ((0, 0), (CONV_PAD_LEFT, CONV_WIDTH - 1 - CONV_PAD_LEFT), (0, 0)))
    y = b
    for k in range(CONV_WIDTH):
        y = y + up[:, k:k + L] * w[k]
    return y


def _lin_combine(earlier, later):
    a1, b1 = earlier
    a2, b2 = later
    return (a1 * a2, a2 * b1 + b2)


def rg_lru_bidir(u, conv_w, conv_b, w_r, b_r, w_i, b_i, lam):
    B, L, _ = u.shape
    xc = short_conv(u, conv_w, conv_b)
    xb = xc.reshape(B, L, LRU_BLOCKS, LRU_BLOCK_DIM)
    xf = xc.astype(jnp.float32)
    hs = []
    for d, rev in enumerate((False, True)):
        r = jax.nn.sigmoid((jnp.einsum('blhc,hcd->blhd', xb, w_r[d]).reshape(B, L, LRU_WIDTH) + b_r[d]).astype(jnp.float32))
        i = jax.nn.sigmoid((jnp.einsum('blhc,hcd->blhd', xb, w_i[d]).reshape(B, L, LRU_WIDTH) + b_i[d]).astype(jnp.float32))
        log_a = LRU_C * r * jax.nn.log_sigmoid(lam[d].astype(jnp.float32))
        a = jnp.exp(log_a)
        inp = jnp.sqrt(-jnp.expm1(2.0 * log_a)) * (i * xf)
        _, h = lax.associative_scan(_lin_combine, (a, inp), reverse=rev, axis=1)
        hs.append(h)
    return (hs[0] + hs[1]).astype(u.dtype)


def axial_rope_tables(L):
    rows = L // GRID_W
    row = jnp.repeat(jnp.arange(rows), GRID_W).astype(jnp.float32)
    col = jnp.tile(jnp.arange(GRID_W), rows).astype(jnp.float32)
    inv = ROPE_BASE ** (-jnp.arange(0, ROPE_AXIS_DIM, 2, dtype=jnp.float32) / ROPE_AXIS_DIM)
    ang = jnp.stack([row[:, None] * inv, col[:, None] * inv], axis=1)
    return jnp.cos(ang), jnp.sin(ang)


def apply_axial_rope(x, cos, sin):
    B, L, H, D = x.shape
    xs = x.astype(jnp.float32).reshape(B, L, H, 2, 2, ROPE_AXIS_DIM // 2)
    x1, x2 = xs[..., 0, :], xs[..., 1, :]
    c, s = cos[:, None], sin[:, None]
    out = jnp.stack([x1 * c - x2 * s, x2 * c + x1 * s], axis=-2)
    return out.reshape(B, L, H, D).astype(x.dtype)


def block_attention(q, k, v):
    B, L = q.shape[0], q.shape[1]
    nb = L // Q_BLOCK
    qb = q.reshape(B, nb, Q_BLOCK, N_KV_HEADS, GQA_GROUP, HEAD_DIM).transpose(1, 0, 2, 3, 4, 5)
    scale = HEAD_DIM ** -0.5

    def one_block(qblk):
        s = jnp.einsum('bqkgd,bskd->bkgqs', qblk, k, preferred_element_type=jnp.float32) * scale
        p = jax.nn.softmax(s, axis=-1)
        return jnp.einsum('bkgqs,bskd->bqkgd', p.astype(v.dtype), v)

    o = lax.map(one_block, qb)
    return o.transpose(1, 0, 2, 3, 4, 5).reshape(B, L, ATTN_WIDTH)


def setup_inputs(seed: int = 0) -> dict:
    key = jax.random.key(seed)
    ks = jax.random.split(key, 16)
    f32 = jnp.float32
    nrm = lambda k, shape, s: jax.random.normal(k, shape, f32) * s
    u = jax.random.uniform(ks[11], (DEPTH, 2, LRU_WIDTH), f32, 0.9, 0.999)
    sg = u ** (1.0 / LRU_C)
    lru_lam = jnp.log(sg) - jnp.log1p(-sg)
    return {
        "x": nrm(ks[0], (BATCH, SEQ, D_MODEL), 1.0),
        "norm_g": 1.0 + nrm(ks[1], (DEPTH, D_MODEL), 0.02),
        "w_in": nrm(ks[2], (DEPTH, D_MODEL, D_IN), D_MODEL ** -0.5),
        "pool_w": nrm(ks[3], (DEPTH, POOL_GROUPS, POOL_GROUP_DIM, POOL_GROUP_DIM), POOL_GROUP_DIM ** -0.5),
        "pool_scale": 1.0 + nrm(ks[4], (DEPTH, POOL_WIDTH), 0.02),
        "conv_w": nrm(ks[5], (DEPTH, CONV_WIDTH, LRU_WIDTH), CONV_WIDTH ** -0.5),
        "conv_b": nrm(ks[6], (DEPTH, LRU_WIDTH), 0.01),
        "lru_wr": nrm(ks[7], (DEPTH, 2, LRU_BLOCKS, LRU_BLOCK_DIM, LRU_BLOCK_DIM), LRU_BLOCK_DIM ** -0.5),
        "lru_br": nrm(ks[8], (DEPTH, 2, LRU_WIDTH), 0.01),
        "lru_wi": nrm(ks[9], (DEPTH, 2, LRU_BLOCKS, LRU_BLOCK_DIM, LRU_BLOCK_DIM), LRU_BLOCK_DIM ** -0.5),
        "lru_bi": nrm(ks[10], (DEPTH, 2, LRU_WIDTH), 0.01),
        "lru_lam": lru_lam,
        "q_norm": 1.0 + nrm(ks[12], (DEPTH, HEAD_DIM), 0.02),
        "k_norm": 1.0 + nrm(ks[13], (DEPTH, HEAD_DIM), 0.02),
        "w_out": nrm(ks[14], (DEPTH, D_MIX, D_MODEL), D_MIX ** -0.5),
    }


def reference(x, norm_g, w_in, pool_w, pool_scale, conv_w, conv_b, lru_wr, lru_br, lru_wi, lru_bi,
              lru_lam, q_norm, k_norm, w_out):
    B, L, _ = x.shape
    cos, sin = axial_rope_tables(L)
    splits = []
    acc = 0
    for s in IN_SIZES[:-1]:
        acc += s
        splits.append(acc)
    for l in range(DEPTH):
        h = rmsnorm(x, norm_g[l])
        z = jnp.einsum('bld,de->ble', h, w_in[l])
        u_pool, g_pool, u_lru, g_lru, q, k, v, g_attn = jnp.split(z, splits, axis=-1)

        y_pool = pool_mixer(u_pool, pool_w[l], pool_scale[l]) * jax.nn.silu(g_pool)

        y_lru = rg_lru_bidir(u_lru, conv_w[l], conv_b[l], lru_wr[l], lru_br[l], lru_wi[l], lru_bi[l],
                             lru_lam[l]) * jax.nn.silu(g_lru)

        q = rmsnorm(q.reshape(B, L, N_Q_HEADS, HEAD_DIM), q_norm[l])
        k = rmsnorm(k.reshape(B, L, N_KV_HEADS, HEAD_DIM), k_norm[l])
        v = v.reshape(B, L, N_KV_HEADS, HEAD_DIM)
        q = apply_axial_rope(q, cos, sin)
        k = apply_axial_rope(k, cos, sin)
        y_attn = block_attention(q, k, v) * jax.nn.silu(g_attn)

        y = jnp.concatenate([y_pool, y_lru, y_attn], axis=-1)
        x = x + jnp.einsum('ble,ed->bld', y, w_out[l])
    return x
```

```python
import functools

import jax
import jax.numpy as jnp
from jax import lax
from jax.experimental import pallas as pl
from jax.experimental.pallas import tpu as pltpu

F32 = jnp.float32
BF16 = jnp.bfloat16

GRID_W = 64
POOL_WINDOWS = (2, 4, 8, 16)
HEAD_DIM = 128
N_Q_HEADS = 6
N_KV_HEADS = 2
GQA_GROUP = N_Q_HEADS // N_KV_HEADS
LRU_BLOCK_DIM = 128
LRU_C = 8.0
CONV_WIDTH = 4
CONV_PAD_LEFT = 1
ROPE_AXIS_DIM = HEAD_DIM // 2
ROPE_BASE = 10000.0
EPS = 1e-6

LANES = 128
SUBLANES = 8
BF16_ROWS = 16
VMEM_LIMIT_BYTES = 56 * 1024 * 1024

ROW_TILE = 512
Q_TILE = 256
KV_TILE = 512
LRU_TIME_TILE = 256
SCAN_PITCH_PAD = 4


def _sigmoid(x):
    return 0.5 * (jnp.tanh(0.5 * x) + 1.0)


def _silu(x):
    return x * _sigmoid(x)


def _compiler_params(semantics):
    return pltpu.CompilerParams(dimension_semantics=semantics, vmem_limit_bytes=VMEM_LIMIT_BYTES)


def _inproj_kernel(x_ref, g_ref, w_ref, qn_ref, kn_ref, cos_ref, sin_ref,
                   zp_ref, zl_ref, q_ref, k_ref, v_ref, ga_ref, *, widths):
    pool2, lru2, qw, kw, vw, gw = widths
    x = x_ref[...]
    ms = jnp.mean(x * x, axis=-1, keepdims=True)
    h = (x * lax.rsqrt(ms + EPS) * g_ref[...]).astype(BF16)

    def proj(lo, width):
        return jnp.dot(h, w_ref[:, lo:lo + width], preferred_element_type=F32)

    off = 0
    zp_ref[...] = proj(off, pool2).astype(BF16)
    off += pool2
    zl_ref[...] = proj(off, lru2).astype(BF16)
    off += lru2

    cos = cos_ref[...]
    sin = sin_ref[...]
    lane = lax.broadcasted_iota(jnp.int32, cos.shape, 1)
    low_half = (lane & (ROPE_AXIS_DIM // 2)) == 0

    def norm_rope(zh, gain, scale):
        msh = jnp.mean(zh * zh, axis=-1, keepdims=True)
        y = zh * lax.rsqrt(msh + EPS) * gain
        partner = jnp.where(low_half,
                            pltpu.roll(y, HEAD_DIM - ROPE_AXIS_DIM // 2, 1),
                            pltpu.roll(y, ROPE_AXIS_DIM // 2, 1))
        return (y * cos + partner * sin) * scale

    zq = proj(off, qw)
    off += qw
    for hd in range(qw // HEAD_DIM):
        sl = slice(hd * HEAD_DIM, (hd + 1) * HEAD_DIM)
        q_ref[:, sl] = norm_rope(zq[:, sl], qn_ref[...], HEAD_DIM ** -0.5).astype(BF16)
    zk = proj(off, kw)
    off += kw
    for hd in range(kw // HEAD_DIM):
        sl = slice(hd * HEAD_DIM, (hd + 1) * HEAD_DIM)
        k_ref[:, sl] = norm_rope(zk[:, sl], kn_ref[...], 1.0).astype(BF16)
    v_ref[...] = proj(off, vw).astype(BF16)
    off += vw
    ga_ref[...] = proj(off, gw).astype(BF16)


def _inproj(x2, norm_g, w_in, q_norm, k_norm, cos_t, sin_t, seq_len):
    n_rows, d_model = x2.shape
    d_in = w_in.shape[1]
    d_mix = 2 * d_model
    pool_w, lru_w, attn_w = d_mix // 4, 3 * d_mix // 8, 3 * d_mix // 8
    kv_w = N_KV_HEADS * HEAD_DIM
    widths = (2 * pool_w, 2 * lru_w, attn_w, kv_w, kv_w, attn_w)
    assert sum(widths) == d_in
    tm = ROW_TILE
    assert n_rows % tm == 0 and seq_len % tm == 0
    tiles_per_seq = seq_len // tm

    row = lambda w: pl.BlockSpec((tm, w), lambda i: (i, 0))
    const = lambda shape: pl.BlockSpec(shape, lambda i: (0,) * len(shape))
    pos = pl.BlockSpec((tm, HEAD_DIM), lambda i: (i % tiles_per_seq, 0))
    out_shape = tuple(jax.ShapeDtypeStruct((n_rows, w), BF16) for w in widths)
    return pl.pallas_call(
        functools.partial(_inproj_kernel, widths=widths),
        grid=(n_rows // tm,),
        in_specs=[row(d_model), const((1, d_model)), const((d_model, d_in)),
                  const((1, HEAD_DIM)), const((1, HEAD_DIM)), pos, pos],
        out_specs=tuple(row(w) for w in widths),
        out_shape=out_shape,
        compiler_params=_compiler_params(("parallel",)),
        name="inproj",
    )(x2, norm_g.reshape(1, d_model), w_in, q_norm.reshape(1, HEAD_DIM),
      k_norm.reshape(1, HEAD_DIM), cos_t, sin_t)


def _lru_kernel(u_ref, g_ref, cw_ref, cb_ref, wr_ref, br_ref, wi_ref, bi_ref, lam_ref, y_ref,
                ubuf, a_f, b_f, a_b, b_b, *, seq_len, pitch, tc):
    n_chunks = seq_len // tc
    pad_rows = SUBLANES * pitch - seq_len
    halo = SUBLANES

    ubuf[0:halo, :] = jnp.zeros((halo, LANES), F32)
    ubuf[halo + seq_len:2 * halo + seq_len, :] = jnp.zeros((halo, LANES), F32)

    def stage(c, _):
        r0 = pl.multiple_of(c * tc, tc)
        ubuf[pl.ds(halo + r0, tc), :] = u_ref[0, pl.ds(r0, tc), :].astype(F32)
        return 0
    lax.fori_loop(0, n_chunks, stage, 0)

    for a_ref, b_ref in ((a_f, b_f), (a_b, b_b)):
        a_ref[seq_len:seq_len + pad_rows, :] = jnp.ones((pad_rows, LANES), F32)
        b_ref[seq_len:seq_len + pad_rows, :] = jnp.zeros((pad_rows, LANES), F32)

    lam = lam_ref[...]
    log_sig_lam = jnp.minimum(lam, 0.0) - jnp.log1p(jnp.exp(-jnp.abs(lam)))
    cw = cw_ref[...]
    cb = cb_ref[...]

    def gates(c, _):
        r0 = pl.multiple_of(c * tc, tc)
        xc = cb
        for k in range(CONV_WIDTH):
            xk = ubuf[pl.ds(halo + r0 + (k - CONV_PAD_LEFT), tc), :]
            xc = xc + xk * cw[k:k + 1, :]
        xcb = xc.astype(BF16)
        for d, (a_ref, b_ref) in enumerate(((a_f, b_f), (a_b, b_b))):
            r = _sigmoid(jnp.dot(xcb, wr_ref[d, 0], preferred_element_type=F32) + br_ref[d:d + 1, :])
            i = _sigmoid(jnp.dot(xcb, wi_ref[d, 0], preferred_element_type=F32) + bi_ref[d:d + 1, :])
            log_a = LRU_C * r * log_sig_lam[d:d + 1, :]
            a = jnp.exp(log_a)
            gain = jnp.sqrt(jnp.tanh(-log_a) * (1.0 + a * a))
            a_ref[pl.ds(r0, tc), :] = a
            b_ref[pl.ds(r0, tc), :] = gain * (i * xc)
        return 0
    lax.fori_loop(0, n_chunks, gates, 0)

    def strided(ref, t):
        return ref[pl.ds(t, SUBLANES, stride=pitch), :]

    def sweep_totals(t, carry):
        pf, hf, pb, hb = carry
        tb = pitch - 1 - t
        af = strided(a_f, t)
        ab = strided(a_b, tb)
        hf = af * hf + strided(b_f, t)
        hb = ab * hb + strided(b_b, tb)
        return pf * af, hf, pb * ab, hb

    ones = jnp.ones((SUBLANES, LANES), F32)
    zeros = jnp.zeros((SUBLANES, LANES), F32)
    pf, hf, pb, hb = lax.fori_loop(0, pitch, sweep_totals, (ones, zeros, ones, zeros), unroll=4)

    cf = [jnp.zeros((1, LANES), F32)]
    for s in range(1, SUBLANES):
        cf.append(pf[s - 1:s, :] * cf[-1] + hf[s - 1:s, :])
    cbk = [jnp.zeros((1, LANES), F32)]
    for s in range(SUBLANES - 2, -1, -1):
        cbk.append(pb[s + 1:s + 2, :] * cbk[-1] + hb[s + 1:s + 2, :])
    cf = jnp.concatenate(cf, axis=0)
    cbk = jnp.concatenate(cbk[::-1], axis=0)

    def sweep_states(t, carry):
        hf, hb = carry
        tb = pitch - 1 - t
        hf = strided(a_f, t) * hf + strided(b_f, t)
        hb = strided(a_b, tb) * hb + strided(b_b, tb)
        b_f[pl.ds(t, SUBLANES, stride=pitch), :] = hf
        b_b[pl.ds(tb, SUBLANES, stride=pitch), :] = hb
        return hf, hb
    lax.fori_loop(0, pitch, sweep_states, (cf, cbk), unroll=4)

    def emit(c, _):
        r0 = pl.multiple_of(c * tc, tc)
        h = b_f[pl.ds(r0, tc), :] + b_b[pl.ds(r0, tc), :]
        g = g_ref[0, pl.ds(r0, tc), :].astype(F32)
        y_ref[0, pl.ds(r0, tc), :] = (h * _silu(g)).astype(BF16)
        return 0
    lax.fori_loop(0, n_chunks, emit, 0)


def _rg_lru(zl3, conv_w, conv_b, w_r, b_r, w_i, b_i, lam):
    batch, seq_len, lru2 = zl3.shape
    lru_w = lru2 // 2
    n_blk = lru_w // LRU_BLOCK_DIM
    assert seq_len % LRU_TIME_TILE == 0
    pitch = seq_len // SUBLANES + SCAN_PITCH_PAD
    rows = SUBLANES * pitch

    col = lambda r: pl.BlockSpec((r, LANES), lambda b, j: (0, j))
    seq = lambda off: pl.BlockSpec((1, seq_len, LANES), lambda b, j: (b, 0, j + off))
    wblk = pl.BlockSpec((2, 1, LRU_BLOCK_DIM, LRU_BLOCK_DIM), lambda b, j: (0, j, 0, 0))
    return pl.pallas_call(
        functools.partial(_lru_kernel, seq_len=seq_len, pitch=pitch, tc=LRU_TIME_TILE),
        grid=(batch, n_blk),
        in_specs=[seq(0), seq(n_blk), col(CONV_WIDTH), col(1), wblk, col(2), wblk, col(2), col(2)],
        out_specs=pl.BlockSpec((1, seq_len, LANES), lambda b, j: (b, 0, j)),
        out_shape=jax.ShapeDtypeStruct((batch, seq_len, lru_w), BF16),
        scratch_shapes=[pltpu.VMEM((seq_len + 2 * SUBLANES, LANES), F32)]
                       + [pltpu.VMEM((rows, LANES), F32)] * 4,
        compiler_params=_compiler_params(("parallel", "parallel")),
        name="rg_lru",
    )(zl3, zl3, conv_w, conv_b.reshape(1, lru_w), w_r.astype(BF16), b_r, w_i.astype(BF16), b_i, lam)


def _attn_kernel(q_ref, k_ref, v_ref, g_ref, o_ref, *, tk):
    tq = q_ref.shape[1]
    seq_len = k_ref.shape[1]
    q = q_ref[0]
    qs = jnp.concatenate([q[:, i * HEAD_DIM:(i + 1) * HEAD_DIM] for i in range(GQA_GROUP)], axis=0)
    rows = GQA_GROUP * tq

    def body(c, carry):
        m, l, acc = carry
        r0 = pl.multiple_of(c * tk, tk)
        kc = k_ref[0, pl.ds(r0, tk), :]
        vc = v_ref[0, pl.ds(r0, tk), :]
        s = lax.dot_general(qs, kc, (((1,), (1,)), ((), ())), preferred_element_type=F32)
        m_new = jnp.maximum(m, jnp.max(s, axis=-1, keepdims=True))
        p = jnp.exp(s - m_new)
        alpha = jnp.exp(m - m_new)
        l = alpha * l + jnp.sum(p, axis=-1, keepdims=True)
        acc = alpha * acc + jnp.dot(p.astype(BF16), vc, preferred_element_type=F32)
        return m_new, l, acc

    init = (jnp.full((rows, 1), -jnp.inf, F32), jnp.zeros((rows, 1), F32),
            jnp.zeros((rows, HEAD_DIM), F32))
    _, l, acc = lax.fori_loop(0, seq_len // tk, body, init)
    o = acc / l
    o = jnp.concatenate([o[i * tq:(i + 1) * tq, :] for i in range(GQA_GROUP)], axis=1)
    o_ref[0] = (o * _silu(g_ref[0].astype(F32))).astype(BF16)


def _attention(q3, k3, v3, g3):
    batch, seq_len, attn_w = q3.shape
    tq, tk = Q_TILE, KV_TILE
    assert seq_len % tq == 0 and seq_len % tk == 0
    gw = GQA_GROUP * HEAD_DIM
    qspec = pl.BlockSpec((1, tq, gw), lambda b, h, i: (b, i, h))
    kvspec = pl.BlockSpec((1, seq_len, HEAD_DIM), lambda b, h, i: (b, 0, h))
    return pl.pallas_call(
        functools.partial(_attn_kernel, tk=tk),
        grid=(batch, N_KV_HEADS, seq_len // tq),
        in_specs=[qspec, kvspec, kvspec, qspec],
        out_specs=qspec,
        out_shape=jax.ShapeDtypeStruct((batch, seq_len, attn_w), BF16),
        compiler_params=_compiler_params(("parallel", "parallel", "parallel")),
        name="attention",
    )(q3, k3, v3, g3)


def _outproj_kernel(x_ref, zp_ref, prev_ref, next_ref, yl_ref, ya_ref, pw_ref, ps_ref, wo_ref,
                    o_ref, *, seq_len, pool_w):
    tm = x_ref.shape[0]
    tiles_per_seq = seq_len // tm
    t0 = (pl.program_id(0) % tiles_per_seq) * tm
    has_prev = (t0 > 0).astype(F32)
    has_next = (t0 + tm < seq_len).astype(F32)
    t = t0 + lax.broadcasted_iota(jnp.int32, (tm, 1), 0)
    ext_rows = tm + 2 * BF16_ROWS

    zp = zp_ref[...]
    groups = []
    for g, w in enumerate(POOL_WINDOWS):
        half = w // 2
        sl = slice(g * LANES, (g + 1) * LANES)
        u = zp[:, sl].astype(F32)
        ext = jnp.concatenate([prev_ref[:, sl].astype(F32) * has_prev, u,
                               next_ref[:, sl].astype(F32) * has_next], axis=0)
        span = 1
        while span < w:
            ext = ext + pltpu.roll(ext, span, 0)
            span *= 2
        if half > 1:
            ext = pltpu.roll(ext, ext_rows - (half - 1), 0)
        win = ext[BF16_ROWS:BF16_ROWS + tm, :]
        cnt = jnp.minimum(t + half, seq_len) - jnp.maximum(t - half, 0)
        pooled = win * (1.0 / cnt.astype(F32)) - u
        mixed = jnp.dot(pooled.astype(BF16), pw_ref[g], preferred_element_type=F32)
        gate = zp[:, pool_w + g * LANES:pool_w + (g + 1) * LANES].astype(F32)
        groups.append((mixed * ps_ref[:, sl] * _silu(gate)).astype(BF16))
    y_pool = jnp.concatenate(groups, axis=1)

    lru_w = yl_ref.shape[1]
    acc = x_ref[...]
    acc = acc + jnp.dot(y_pool, wo_ref[0:pool_w, :], preferred_element_type=F32)
    acc = acc + jnp.dot(yl_ref[...], wo_ref[pool_w:pool_w + lru_w, :], preferred_element_type=F32)
    acc = acc + jnp.dot(ya_ref[...], wo_ref[pool_w + lru_w:, :], preferred_element_type=F32)
    o_ref[...] = acc


def _outproj(x2, zp, y_lru, y_attn, pool_wt, pool_scale, w_out, seq_len):
    n_rows, d_model = x2.shape
    d_mix = w_out.shape[0]
    pool_w = d_mix // 4
    lru_w = y_lru.shape[1]
    attn_w = y_attn.shape[1]
    tm = ROW_TILE
    halo_blocks = tm // BF16_ROWS
    last_halo = n_rows // BF16_ROWS - 1

    row = lambda w: pl.BlockSpec((tm, w), lambda i: (i, 0))
    const = lambda shape: pl.BlockSpec(shape, lambda i: (0,) * len(shape))
    prev = pl.BlockSpec((BF16_ROWS, pool_w), lambda i: (jnp.maximum(i * halo_blocks - 1, 0), 0))
    nxt = pl.BlockSpec((BF16_ROWS, pool_w),
                       lambda i: (jnp.minimum((i + 1) * halo_blocks, last_halo), 0))
    return pl.pallas_call(
        functools.partial(_outproj_kernel, seq_len=seq_len, pool_w=pool_w),
        grid=(n_rows // tm,),
        in_specs=[row(d_model), row(2 * pool_w), prev, nxt, row(lru_w), row(attn_w),
                  const(pool_wt.shape), const((1, pool_w)), const((d_mix, d_model))],
        out_specs=row(d_model),
        out_shape=jax.ShapeDtypeStruct((n_rows, d_model), F32),
        compiler_params=_compiler_params(("parallel",)),
        name="outproj",
    )(x2, zp, zp, zp, y_lru, y_attn, pool_wt, pool_scale.reshape(1, pool_w), w_out)


def _rope_tables(seq_len):
    t = jnp.arange(seq_len)
    row = (t // GRID_W).astype(F32)
    col = (t % GRID_W).astype(F32)
    inv = ROPE_BASE ** (-jnp.arange(0, ROPE_AXIS_DIM, 2, dtype=F32) / ROPE_AXIS_DIM)
    ang_r = row[:, None] * inv
    ang_c = col[:, None] * inv
    cos_t = jnp.concatenate([jnp.cos(ang_r)] * 2 + [jnp.cos(ang_c)] * 2, axis=1)
    sin_t = jnp.concatenate([-jnp.sin(ang_r), jnp.sin(ang_r), -jnp.sin(ang_c), jnp.sin(ang_c)], axis=1)
    return cos_t, sin_t


def kernel(x, norm_g, w_in, pool_w, pool_scale, conv_w, conv_b, lru_wr, lru_br, lru_wi, lru_bi,
           lru_lam, q_norm, k_norm, w_out):
    batch, seq_len, d_model = x.shape
    depth = w_in.shape[0]
    n_rows = batch * seq_len
    cos_t, sin_t = _rope_tables(seq_len)
    w_in_b = w_in.astype(BF16)
    w_out_b = w_out.astype(BF16)
    pool_w_b = pool_w.astype(BF16)

    x2 = x.reshape(n_rows, d_model)
    for l in range(depth):
        zp, zl, q, k, v, ga = _inproj(x2, norm_g[l], w_in_b[l], q_norm[l], k_norm[l], cos_t, sin_t,
                                      seq_len)
        seq3 = lambda a: a.reshape(batch, seq_len, a.shape[-1])
        y_lru = _rg_lru(seq3(zl), conv_w[l], conv_b[l], lru_wr[l], lru_br[l], lru_wi[l], lru_bi[l],
                        lru_lam[l])
        y_attn = _attention(seq3(q), seq3(k), seq3(v), seq3(ga))
        x2 = _outproj(x2, zp, y_lru.reshape(n_rows, -1), y_attn.reshape(n_rows, -1),
                      pool_w_b[l], pool_scale[l], w_out_b[l], seq_len)
    return x2.reshape(batch, seq_len, d_model)
```

```python
import functools

import jax
import jax.numpy as jnp
from jax import lax
from jax.experimental import pallas as pl
from jax.experimental.pallas import tpu as pltpu

F32 = jnp.float32
BF16 = jnp.bfloat16

GRID_W = 64
POOL_WINDOWS = (2, 4, 8, 16)
HEAD_DIM = 128
N_Q_HEADS = 6
N_KV_HEADS = 2
GQA_GROUP = N_Q_HEADS // N_KV_HEADS
LRU_BLOCK_DIM = 128
LRU_C = 8.0
CONV_WIDTH = 4
CONV_PAD_LEFT = 1
ROPE_AXIS_DIM = HEAD_DIM // 2
ROPE_BASE = 10000.0
EPS = 1e-6
Q_SCALE = HEAD_DIM ** -0.5 * 1.4426950408889634

LANES = 128
SUBLANES = 8
BF16_ROWS = 16
VMEM_LIMIT_BYTES = 56 * 1024 * 1024

ROW_TILE = 512
Q_TILE = 512
KV_TILE = 512
ATTN_ROW_BLOCK = 32
LRU_TIME_TILE = 512
SCAN_GROUPS = 4
SCAN_PITCH_PAD = 4


def _sigmoid(x):
    return 0.5 * (jnp.tanh(0.5 * x) + 1.0)


def _silu(x):
    return x * _sigmoid(x)


def _compiler_params(semantics):
    return pltpu.CompilerParams(dimension_semantics=semantics, vmem_limit_bytes=VMEM_LIMIT_BYTES)


def _inproj_kernel(x_ref, g_ref, w_ref, qn_ref, kn_ref, cos_ref, sin_ref,
                   zp_ref, zl_ref, q_ref, k_ref, v_ref, ga_ref, *, widths):
    pool2, lru2, qw, kw, vw, gw = widths
    x = x_ref[...]
    ms = jnp.mean(x * x, axis=-1, keepdims=True)
    h = (x * lax.rsqrt(ms + EPS) * g_ref[...]).astype(BF16)

    def proj(lo, width):
        return jnp.dot(h, w_ref[:, lo:lo + width], preferred_element_type=F32)

    cos = cos_ref[...]
    sin = sin_ref[...]

    def norm_rope(zh, gain, scale):
        msh = jnp.mean(zh * zh, axis=-1, keepdims=True)
        y = zh * lax.rsqrt(msh + EPS) * gain
        partner = pltpu.roll(y, HEAD_DIM // 2, 1)
        return (y * cos + partner * sin) * scale

    q_off = pool2 + lru2
    zq = proj(q_off, qw)
    for hd in range(qw // HEAD_DIM):
        sl = slice(hd * HEAD_DIM, (hd + 1) * HEAD_DIM)
        q_ref[:, sl] = norm_rope(zq[:, sl], qn_ref[...], Q_SCALE).astype(BF16)
    zk = proj(q_off + qw, kw)
    for hd in range(kw // HEAD_DIM):
        sl = slice(hd * HEAD_DIM, (hd + 1) * HEAD_DIM)
        k_ref[:, sl] = norm_rope(zk[:, sl], kn_ref[...], 1.0).astype(BF16)
    v_ref[...] = proj(q_off + qw + kw, vw).astype(BF16)
    ga_ref[...] = proj(q_off + qw + kw + vw, gw).astype(BF16)
    zp_ref[...] = proj(0, pool2).astype(BF16)
    zl_ref[...] = proj(pool2, lru2).astype(BF16)


def _inproj(x2, norm_g, w_in, q_norm, k_norm, cos_t, sin_t, seq_len):
    n_rows, d_model = x2.shape
    d_in = w_in.shape[1]
    d_mix = 2 * d_model
    pool_w, lru_w, attn_w = d_mix // 4, 3 * d_mix // 8, 3 * d_mix // 8
    kv_w = N_KV_HEADS * HEAD_DIM
    widths = (2 * pool_w, 2 * lru_w, attn_w, kv_w, kv_w, attn_w)
    assert sum(widths) == d_in
    tm = ROW_TILE
    assert n_rows % tm == 0 and seq_len % tm == 0
    tiles_per_seq = seq_len // tm

    row = lambda w: pl.BlockSpec((tm, w), lambda i: (i, 0))
    const = lambda shape: pl.BlockSpec(shape, lambda i: (0,) * len(shape))
    pos = pl.BlockSpec((tm, HEAD_DIM), lambda i: (i % tiles_per_seq, 0))
    out_shape = tuple(jax.ShapeDtypeStruct((n_rows, w), BF16) for w in widths)
    return pl.pallas_call(
        functools.partial(_inproj_kernel, widths=widths),
        grid=(n_rows // tm,),
        in_specs=[row(d_model), const((1, d_model)), const((d_model, d_in)),
                  const((1, HEAD_DIM)), const((1, HEAD_DIM)), pos, pos],
        out_specs=tuple(row(w) for w in widths),
        out_shape=out_shape,
        compiler_params=_compiler_params(("parallel",)),
        name="inproj",
    )(x2, norm_g.reshape(1, d_model), w_in, q_norm.reshape(1, HEAD_DIM),
      k_norm.reshape(1, HEAD_DIM), cos_t, sin_t)


def _lru_kernel(u_ref, g_ref, cw_ref, cb_ref, wr_ref, br_ref, wi_ref, bi_ref, lam_ref, y_ref,
                ubuf, a_f, b_f, a_b, b_b, h_f, h_b, *, seq_len, pitch, n_groups, tc):
    n_chunks = seq_len // tc
    pad_rows = n_groups * SUBLANES * pitch - seq_len
    halo = SUBLANES

    ubuf[0:halo, :] = jnp.zeros((halo, LANES), F32)
    ubuf[halo + seq_len:2 * halo + seq_len, :] = jnp.zeros((halo, LANES), F32)

    def stage(c, _):
        r0 = pl.multiple_of(c * tc, tc)
        ubuf[pl.ds(halo + r0, tc), :] = u_ref[0, pl.ds(r0, tc), :].astype(F32)
        return 0
    lax.fori_loop(0, n_chunks, stage, 0)

    for a_ref, b_ref in ((a_f, b_f), (a_b, b_b)):
        a_ref[seq_len:seq_len + pad_rows, :] = jnp.ones((pad_rows, LANES), F32)
        b_ref[seq_len:seq_len + pad_rows, :] = jnp.zeros((pad_rows, LANES), F32)

    lam = lam_ref[...]
    log_sig_lam = jnp.minimum(lam, 0.0) - jnp.log1p(jnp.exp(-jnp.abs(lam)))
    half_c = (0.5 * LRU_C) * log_sig_lam
    cw = cw_ref[...]
    cb = cb_ref[...]
    br_half = 0.5 * br_ref[...]
    bi_half = 0.5 * bi_ref[...]

    def gates(c, _):
        r0 = pl.multiple_of(c * tc, tc)
        xc = cb
        for k in range(CONV_WIDTH):
            xk = ubuf[pl.ds(halo + r0 + (k - CONV_PAD_LEFT), tc), :]
            xc = xc + xk * cw[k:k + 1, :]
        xcb = xc.astype(BF16)
        half_x = 0.5 * xc
        for d, (a_ref, b_ref) in enumerate(((a_f, b_f), (a_b, b_b))):
            tr = jnp.tanh(jnp.dot(xcb, wr_ref[d, 0], preferred_element_type=F32) + br_half[d:d + 1, :])
            ti = jnp.tanh(jnp.dot(xcb, wi_ref[d, 0], preferred_element_type=F32) + bi_half[d:d + 1, :])
            log_a = tr * half_c[d:d + 1, :] + half_c[d:d + 1, :]
            a = jnp.exp(log_a)
            sq = jnp.tanh(log_a) * (-1.0 - a * a)
            gain = jnp.where(sq > 0.0, sq * lax.rsqrt(sq), 0.0)
            a_ref[pl.ds(r0, tc), :] = a
            b_ref[pl.ds(r0, tc), :] = gain * ((ti + 1.0) * half_x)
        return 0
    lax.fori_loop(0, n_chunks, gates, 0)

    n_seq_chunks = n_groups * SUBLANES

    def strided(g, t):
        return pl.ds(g * SUBLANES * pitch + t, SUBLANES, stride=pitch)

    def sweep_totals(t, carry):
        tb = pitch - 1 - t
        out = []
        for g, (pf, hf, pb, hb) in enumerate(carry):
            af = a_f[strided(g, t), :]
            ab = a_b[strided(g, tb), :]
            out.append((pf * af, af * hf + b_f[strided(g, t), :],
                        pb * ab, ab * hb + b_b[strided(g, tb), :]))
        return tuple(out)

    ones = jnp.ones((SUBLANES, LANES), F32)
    zeros = jnp.zeros((SUBLANES, LANES), F32)
    totals = lax.fori_loop(0, pitch, sweep_totals, ((ones, zeros, ones, zeros),) * n_groups,
                           unroll=2)

    def chunk_row(field, k):
        s = k % SUBLANES
        return totals[k // SUBLANES][field][s:s + 1, :]
    cf = [jnp.zeros((1, LANES), F32)]
    for k in range(1, n_seq_chunks):
        cf.append(chunk_row(0, k - 1) * cf[-1] + chunk_row(1, k - 1))
    cbk = [jnp.zeros((1, LANES), F32)]
    for k in range(n_seq_chunks - 2, -1, -1):
        cbk.append(chunk_row(2, k + 1) * cbk[-1] + chunk_row(3, k + 1))
    cbk = cbk[::-1]
    entry = tuple((jnp.concatenate(cf[g * SUBLANES:(g + 1) * SUBLANES], axis=0),
                   jnp.concatenate(cbk[g * SUBLANES:(g + 1) * SUBLANES], axis=0))
                  for g in range(n_groups))

    def sweep_states(t, carry):
        tb = pitch - 1 - t
        out = []
        for g, (hf, hb) in enumerate(carry):
            hf = a_f[strided(g, t), :] * hf + b_f[strided(g, t), :]
            hb = a_b[strided(g, tb), :] * hb + b_b[strided(g, tb), :]
            h_f[strided(g, t), :] = hf
            h_b[strided(g, tb), :] = hb
            out.append((hf, hb))
        return tuple(out)
    lax.fori_loop(0, pitch, sweep_states, entry, unroll=2)

    def emit(c, _):
        r0 = pl.multiple_of(c * tc, tc)
        h = h_f[pl.ds(r0, tc), :] + h_b[pl.ds(r0, tc), :]
        g = g_ref[0, pl.ds(r0, tc), :].astype(F32)
        y_ref[0, pl.ds(r0, tc), :] = (h * _silu(g)).astype(BF16)
        return 0
    lax.fori_loop(0, n_chunks, emit, 0)


def _rg_lru(zl3, conv_w, conv_b, w_r, b_r, w_i, b_i, lam):
    batch, seq_len, lru2 = zl3.shape
    lru_w = lru2 // 2
    n_blk = lru_w // LRU_BLOCK_DIM
    n_seq_chunks = SCAN_GROUPS * SUBLANES
    assert seq_len % LRU_TIME_TILE == 0 and seq_len % n_seq_chunks == 0
    pitch = seq_len // n_seq_chunks + SCAN_PITCH_PAD
    rows = n_seq_chunks * pitch

    col = lambda r: pl.BlockSpec((r, LANES), lambda b, j: (0, j))
    seq = lambda off: pl.BlockSpec((1, seq_len, LANES), lambda b, j: (b, 0, j + off))
    wblk = pl.BlockSpec((2, 1, LRU_BLOCK_DIM, LRU_BLOCK_DIM), lambda b, j: (0, j, 0, 0))
    return pl.pallas_call(
        functools.partial(_lru_kernel, seq_len=seq_len, pitch=pitch, n_groups=SCAN_GROUPS,
                          tc=LRU_TIME_TILE),
        grid=(batch, n_blk),
        in_specs=[seq(0), seq(n_blk), col(CONV_WIDTH), col(1), wblk, col(2), wblk, col(2), col(2)],
        out_specs=pl.BlockSpec((1, seq_len, LANES), lambda b, j: (b, 0, j)),
        out_shape=jax.ShapeDtypeStruct((batch, seq_len, lru_w), BF16),
        scratch_shapes=[pltpu.VMEM((seq_len + 2 * SUBLANES, LANES), F32)]
                       + [pltpu.VMEM((rows, LANES), F32)] * 6,
        compiler_params=_compiler_params(("parallel", "parallel")),
        name="rg_lru",
    )(zl3, zl3, conv_w, conv_b.reshape(1, lru_w), (0.5 * w_r).astype(BF16), b_r,
      (0.5 * w_i).astype(BF16), b_i, lam)


def _attn_kernel(q_ref, k_ref, v_ref, g_ref, o_ref, s_a, s_b, p_a, p_b, al_a, al_b,
                 m_ref, l_ref, acc_ref, *, tk, rb):
    tq = q_ref.shape[1]
    seq_len = k_ref.shape[1]
    rows = GQA_GROUP * tq
    n_chunks = seq_len // tk
    q = q_ref[0]
    qs = jnp.concatenate([q[:, i * HEAD_DIM:(i + 1) * HEAD_DIM] for i in range(GQA_GROUP)], axis=0)

    def scores(c, s_ref):
        r0 = pl.multiple_of(c * tk, tk)
        kc = k_ref[0, pl.ds(r0, tk), :]
        s_ref[...] = lax.dot_general(qs, kc, (((1,), (1,)), ((), ())), preferred_element_type=F32)

    def softmax(s_ref, p_ref, al_ref):
        for r in range(rows // rb):
            rs = slice(r * rb, (r + 1) * rb)
            s = [s_ref[rs, j * LANES:(j + 1) * LANES] for j in range(tk // LANES)]
            mx = functools.reduce(jnp.maximum, s)
            m_old = m_ref[rs, :]
            m_new = jnp.maximum(m_old, jnp.max(mx, axis=-1, keepdims=True))
            alpha = jnp.exp2(m_old - m_new)
            p = [jnp.exp2(sj - m_new) for sj in s]
            l_ref[rs, :] = alpha * l_ref[rs, :] + functools.reduce(jnp.add, p)
            for j, pj in enumerate(p):
                p_ref[rs, j * LANES:(j + 1) * LANES] = pj.astype(BF16)
            al_ref[rs, :] = alpha
            m_ref[rs, :] = m_new

    def weighted_values(c, p_ref, al_ref):
        r0 = pl.multiple_of(c * tk, tk)
        vc = v_ref[0, pl.ds(r0, tk), :]
        acc_ref[...] = al_ref[...] * acc_ref[...] + jnp.dot(p_ref[...], vc,
                                                             preferred_element_type=F32)

    m_ref[...] = jnp.full(m_ref.shape, -jnp.inf, F32)
    l_ref[...] = jnp.zeros(l_ref.shape, F32)
    acc_ref[...] = jnp.zeros(acc_ref.shape, F32)
    scores(0, s_a)
    scores(1, s_b)
    softmax(s_a, p_a, al_a)

    def body(i, _):
        c = 2 * i
        weighted_values(c, p_a, al_a)
        scores(c + 2, s_a)
        softmax(s_b, p_b, al_b)
        weighted_values(c + 1, p_b, al_b)
        scores(c + 3, s_b)
        softmax(s_a, p_a, al_a)
        return 0
    lax.fori_loop(0, n_chunks // 2 - 1, body, 0)

    weighted_values(n_chunks - 2, p_a, al_a)
    softmax(s_b, p_b, al_b)
    weighted_values(n_chunks - 1, p_b, al_b)

    o = acc_ref[...] / jnp.sum(l_ref[...], axis=-1, keepdims=True)
    o = jnp.concatenate([o[i * tq:(i + 1) * tq, :] for i in range(GQA_GROUP)], axis=1)
    o_ref[0] = (o * _silu(g_ref[0].astype(F32))).astype(BF16)


def _attention(q3, k3, v3, g3):
    batch, seq_len, attn_w = q3.shape
    tq, tk = Q_TILE, KV_TILE
    assert seq_len % tq == 0 and seq_len % (2 * tk) == 0 and seq_len >= 4 * tk
    gw = GQA_GROUP * HEAD_DIM
    rows = GQA_GROUP * tq
    qspec = pl.BlockSpec((1, tq, gw), lambda b, h, i: (b, i, h))
    kvspec = pl.BlockSpec((1, seq_len, HEAD_DIM), lambda b, h, i: (b, 0, h))
    return pl.pallas_call(
        functools.partial(_attn_kernel, tk=tk, rb=ATTN_ROW_BLOCK),
        grid=(batch, N_KV_HEADS, seq_len // tq),
        in_specs=[qspec, kvspec, kvspec, qspec],
        out_specs=qspec,
        out_shape=jax.ShapeDtypeStruct((batch, seq_len, attn_w), BF16),
        scratch_shapes=[pltpu.VMEM((rows, tk), F32)] * 2 + [pltpu.VMEM((rows, tk), BF16)] * 2
                       + [pltpu.VMEM((rows, LANES), F32)] * 4
                       + [pltpu.VMEM((rows, HEAD_DIM), F32)],
        compiler_params=_compiler_params(("parallel", "parallel", "parallel")),
        name="attention",
    )(q3, k3, v3, g3)


def _outproj_kernel(x_ref, zp_ref, prev_ref, next_ref, yl_ref, ya_ref, pw_ref, ps_ref, wo_ref,
                    o_ref, *, seq_len, pool_w):
    tm = x_ref.shape[0]
    tiles_per_seq = seq_len // tm
    t0 = (pl.program_id(0) % tiles_per_seq) * tm
    has_prev = (t0 > 0).astype(F32)
    has_next = (t0 + tm < seq_len).astype(F32)
    t = t0 + lax.broadcasted_iota(jnp.int32, (tm, 1), 0)
    ext_rows = tm + 2 * BF16_ROWS

    lru_w = yl_ref.shape[1]
    acc = jnp.dot(yl_ref[...], wo_ref[pool_w:pool_w + lru_w, :], preferred_element_type=F32)
    acc = acc + jnp.dot(ya_ref[...], wo_ref[pool_w + lru_w:, :], preferred_element_type=F32)

    zp = zp_ref[...]
    groups = []
    for g, w in enumerate(POOL_WINDOWS):
        half = w // 2
        sl = slice(g * LANES, (g + 1) * LANES)
        u = zp[:, sl].astype(F32)
        ext = jnp.concatenate([prev_ref[:, sl].astype(F32) * has_prev, u,
                               next_ref[:, sl].astype(F32) * has_next], axis=0)
        span = 1
        while span < w:
            ext = ext + pltpu.roll(ext, span, 0)
            span *= 2
        if half > 1:
            ext = pltpu.roll(ext, ext_rows - (half - 1), 0)
        win = ext[BF16_ROWS:BF16_ROWS + tm, :]
        cnt = jnp.minimum(t + half, seq_len) - jnp.maximum(t - half, 0)
        pooled = win * (1.0 / cnt.astype(F32)) - u
        mixed = jnp.dot(pooled.astype(BF16), pw_ref[g], preferred_element_type=F32)
        gate = zp[:, pool_w + g * LANES:pool_w + (g + 1) * LANES].astype(F32)
        groups.append((mixed * ps_ref[:, sl] * _silu(gate)).astype(BF16))
    y_pool = jnp.concatenate(groups, axis=1)

    acc = acc + jnp.dot(y_pool, wo_ref[0:pool_w, :], preferred_element_type=F32)
    o_ref[...] = x_ref[...] + acc


def _outproj(x2, zp, y_lru, y_attn, pool_wt, pool_scale, w_out, seq_len):
    n_rows, d_model = x2.shape
    d_mix = w_out.shape[0]
    pool_w = d_mix // 4
    lru_w = y_lru.shape[1]
    attn_w = y_attn.shape[1]
    tm = ROW_TILE
    halo_blocks = tm // BF16_ROWS
    last_halo = n_rows // BF16_ROWS - 1

    row = lambda w: pl.BlockSpec((tm, w), lambda i: (i, 0))
    const = lambda shape: pl.BlockSpec(shape, lambda i: (0,) * len(shape))
    prev = pl.BlockSpec((BF16_ROWS, pool_w), lambda i: (jnp.maximum(i * halo_blocks - 1, 0), 0))
    nxt = pl.BlockSpec((BF16_ROWS, pool_w),
                       lambda i: (jnp.minimum((i + 1) * halo_blocks, last_halo), 0))
    return pl.pallas_call(
        functools.partial(_outproj_kernel, seq_len=seq_len, pool_w=pool_w),
        grid=(n_rows // tm,),
        in_specs=[row(d_model), row(2 * pool_w), prev, nxt, row(lru_w), row(attn_w),
                  const(pool_wt.shape), const((1, pool_w)), const((d_mix, d_model))],
        out_specs=row(d_model),
        out_shape=jax.ShapeDtypeStruct((n_rows, d_model), F32),
        compiler_params=_compiler_params(("parallel",)),
        name="outproj",
    )(x2, zp, zp, zp, y_lru, y_attn, pool_wt, pool_scale.reshape(1, pool_w), w_out)


def _halves_major(a):
    lead = a.shape[:-1]
    heads = a.shape[-1] // HEAD_DIM
    a = a.reshape(lead + (heads, 2, 2, ROPE_AXIS_DIM // 2))
    return jnp.swapaxes(a, -3, -2).reshape(lead + (heads * HEAD_DIM,))


def _rope_tables(seq_len):
    t = jnp.arange(seq_len)
    row = (t // GRID_W).astype(F32)
    col = (t % GRID_W).astype(F32)
    inv = ROPE_BASE ** (-jnp.arange(0, ROPE_AXIS_DIM, 2, dtype=F32) / ROPE_AXIS_DIM)
    ang = jnp.concatenate([row[:, None] * inv, col[:, None] * inv], axis=1)
    cos_t = jnp.concatenate([jnp.cos(ang), jnp.cos(ang)], axis=1)
    sin_t = jnp.concatenate([-jnp.sin(ang), jnp.sin(ang)], axis=1)
    return cos_t, sin_t


def kernel(x, norm_g, w_in, pool_w, pool_scale, conv_w, conv_b, lru_wr, lru_br, lru_wi, lru_bi,
           lru_lam, q_norm, k_norm, w_out):
    batch, seq_len, d_model = x.shape
    depth = w_in.shape[0]
    n_rows = batch * seq_len
    cos_t, sin_t = _rope_tables(seq_len)
    d_mix = 2 * d_model
    qk_lo = d_mix // 2 + 3 * d_mix // 4
    qk_hi = qk_lo + 3 * d_mix // 8 + N_KV_HEADS * HEAD_DIM
    w_in_b = jnp.concatenate([w_in[..., :qk_lo], _halves_major(w_in[..., qk_lo:qk_hi]),
                              w_in[..., qk_hi:]], axis=-1).astype(BF16)
    q_norm_p = _halves_major(q_norm)
    k_norm_p = _halves_major(k_norm)
    w_out_b = w_out.astype(BF16)
    pool_w_b = pool_w.astype(BF16)

    x2 = x.reshape(n_rows, d_model)
    for l in range(depth):
        zp, zl, q, k, v, ga = _inproj(x2, norm_g[l], w_in_b[l], q_norm_p[l], k_norm_p[l], cos_t,
                                      sin_t, seq_len)
        seq3 = lambda a: a.reshape(batch, seq_len, a.shape[-1])
        y_lru = _rg_lru(seq3(zl), conv_w[l], conv_b[l], lru_wr[l], lru_br[l], lru_wi[l], lru_bi[l],
                        lru_lam[l])
        y_attn = _attention(seq3(q), seq3(k), seq3(v), seq3(ga))
        x2 = _outproj(x2, zp, y_lru.reshape(n_rows, -1), y_attn.reshape(n_rows, -1),
                      pool_w_b[l], pool_scale[l], w_out_b[l], seq_len)
    return x2.reshape(batch, seq_len, d_model)
```

```python
import functools

import jax
import jax.numpy as jnp
import numpy as np
from jax import lax
from jax.experimental import pallas as pl
from jax.experimental.pallas import tpu as pltpu

F32 = jnp.float32
BF16 = jnp.bfloat16

GRID_W = 64
POOL_WINDOWS = (2, 4, 8, 16)
HEAD_DIM = 128
N_Q_HEADS = 6
N_KV_HEADS = 2
GQA_GROUP = N_Q_HEADS // N_KV_HEADS
LRU_BLOCK_DIM = 128
LRU_C = 8.0
CONV_WIDTH = 4
CONV_PAD_LEFT = 1
ROPE_AXIS_DIM = HEAD_DIM // 2
ROPE_BASE = 10000.0
EPS = 1e-6
Q_SCALE = HEAD_DIM ** -0.5 * 1.4426950408889634

LANES = 128
SUBLANES = 8
BF16_ROWS = 16
VMEM_LIMIT_BYTES = 56 * 1024 * 1024

ROW_TILE = 512
Q_TILE = 512
KV_TILE = 512
ATTN_ROW_BLOCK = 32
POOL_ROW_CHUNK = 128
LRU_TIME_TILE = 512
SCAN_GROUPS = 4
SCAN_PITCH_PAD = 4


def _silu(x):
    h = 0.5 * x
    return h * jnp.tanh(h) + h


def _compiler_params(semantics):
    return pltpu.CompilerParams(dimension_semantics=semantics, vmem_limit_bytes=VMEM_LIMIT_BYTES)


def _inproj_kernel(x_ref, g_ref, w_ref, qn_ref, kn_ref, cos_ref, sin_ref,
                   zp_ref, zl_ref, q_ref, k_ref, v_ref, ga_ref, *, widths):
    pool2, lru2, qw, kw, vw, gw = widths
    x = x_ref[...]
    ms = jnp.mean(x * x, axis=-1, keepdims=True)
    h = (x * lax.rsqrt(ms + EPS) * g_ref[...]).astype(BF16)

    def proj(lo, width):
        return jnp.dot(h, w_ref[:, lo:lo + width], preferred_element_type=F32)

    cos = cos_ref[...]
    sin = sin_ref[...]
    lane = lax.broadcasted_iota(jnp.int32, cos.shape, 1)
    low_half = (lane & (ROPE_AXIS_DIM // 2)) == 0

    def norm_rope(zh, gain, scale):
        msh = jnp.mean(zh * zh, axis=-1, keepdims=True)
        y = zh * lax.rsqrt(msh + EPS) * gain
        partner = jnp.where(low_half,
                            pltpu.roll(y, HEAD_DIM - ROPE_AXIS_DIM // 2, 1),
                            pltpu.roll(y, ROPE_AXIS_DIM // 2, 1))
        return (y * cos + partner * sin) * scale

    q_off = pool2 + lru2
    zq = proj(q_off, qw)
    for hd in range(qw // HEAD_DIM):
        sl = slice(hd * HEAD_DIM, (hd + 1) * HEAD_DIM)
        q_ref[:, sl] = norm_rope(zq[:, sl], qn_ref[...], Q_SCALE).astype(BF16)
    zk = proj(q_off + qw, kw)
    for hd in range(kw // HEAD_DIM):
        sl = slice(hd * HEAD_DIM, (hd + 1) * HEAD_DIM)
        k_ref[:, sl] = norm_rope(zk[:, sl], kn_ref[...], 1.0).astype(BF16)
    v_ref[...] = proj(q_off + qw + kw, vw).astype(BF16)
    ga_ref[...] = proj(q_off + qw + kw + vw, gw).astype(BF16)
    zp_ref[...] = proj(0, pool2).astype(BF16)
    zl_ref[...] = proj(pool2, lru2).astype(BF16)


def _inproj(x2, norm_g, w_in, layer, q_norm, k_norm, cos_t, sin_t, seq_len):
    n_rows, d_model = x2.shape
    d_in = w_in.shape[-1]
    d_mix = 2 * d_model
    pool_w, lru_w, attn_w = d_mix // 4, 3 * d_mix // 8, 3 * d_mix // 8
    kv_w = N_KV_HEADS * HEAD_DIM
    widths = (2 * pool_w, 2 * lru_w, attn_w, kv_w, kv_w, attn_w)
    assert sum(widths) == d_in
    tm = ROW_TILE
    assert n_rows % tm == 0 and seq_len % tm == 0
    tiles_per_seq = seq_len // tm

    row = lambda w: pl.BlockSpec((tm, w), lambda i: (i, 0))
    const = lambda shape: pl.BlockSpec(shape, lambda i: (0,) * len(shape))
    pos = pl.BlockSpec((tm, HEAD_DIM), lambda i: (i % tiles_per_seq, 0))
    out_shape = tuple(jax.ShapeDtypeStruct((n_rows, w), BF16) for w in widths)
    return pl.pallas_call(
        functools.partial(_inproj_kernel, widths=widths),
        grid=(n_rows // tm,),
        in_specs=[row(d_model), const((1, d_model)),
                  pl.BlockSpec((None, d_model, d_in), lambda i: (layer, 0, 0)),
                  const((1, HEAD_DIM)), const((1, HEAD_DIM)), pos, pos],
        out_specs=tuple(row(w) for w in widths),
        out_shape=out_shape,
        compiler_params=_compiler_params(("parallel",)),
        name="inproj",
    )(x2, norm_g.reshape(1, d_model), w_in, q_norm.reshape(1, HEAD_DIM),
      k_norm.reshape(1, HEAD_DIM), cos_t, sin_t)


def _lru_kernel(u_ref, g_ref, cw_ref, cb_ref, wr_ref, br_ref, wi_ref, bi_ref, lam_ref, y_ref,
                ubuf, a_f, b_f, a_b, b_b, h_f, h_b, *, seq_len, pitch, n_groups, tc):
    n_chunks = seq_len // tc
    pad_rows = n_groups * SUBLANES * pitch - seq_len
    halo = SUBLANES

    ubuf[0:halo, :] = jnp.zeros((halo, LANES), F32)
    ubuf[halo + seq_len:2 * halo + seq_len, :] = jnp.zeros((halo, LANES), F32)

    def stage(c, _):
        r0 = pl.multiple_of(c * tc, tc)
        ubuf[pl.ds(halo + r0, tc), :] = u_ref[0, pl.ds(r0, tc), :].astype(F32)
        return 0
    lax.fori_loop(0, n_chunks, stage, 0)

    for a_ref, b_ref in ((a_f, b_f), (a_b, b_b)):
        a_ref[seq_len:seq_len + pad_rows, :] = jnp.ones((pad_rows, LANES), F32)
        b_ref[seq_len:seq_len + pad_rows, :] = jnp.zeros((pad_rows, LANES), F32)

    lam = lam_ref[...]
    log_sig_lam = jnp.minimum(lam, 0.0) - jnp.log1p(jnp.exp(-jnp.abs(lam)))
    half_c = (0.5 * LRU_C) * log_sig_lam
    cw = cw_ref[...]
    cb = cb_ref[...]
    br_half = 0.5 * br_ref[...]
    bi_half = 0.5 * bi_ref[...]

    def gates(c, _):
        r0 = pl.multiple_of(c * tc, tc)
        xc = cb
        for k in range(CONV_WIDTH):
            xk = ubuf[pl.ds(halo + r0 + (k - CONV_PAD_LEFT), tc), :]
            xc = xc + xk * cw[k:k + 1, :]
        xcb = xc.astype(BF16)
        half_x = 0.5 * xc
        for d, (a_ref, b_ref) in enumerate(((a_f, b_f), (a_b, b_b))):
            tr = jnp.tanh(jnp.dot(xcb, wr_ref[d, 0], preferred_element_type=F32) + br_half[d:d + 1, :])
            ti = jnp.tanh(jnp.dot(xcb, wi_ref[d, 0], preferred_element_type=F32) + bi_half[d:d + 1, :])
            log_a = tr * half_c[d:d + 1, :] + half_c[d:d + 1, :]
            a = jnp.exp(log_a)
            sq = jnp.tanh(log_a) * (-1.0 - a * a)
            gain = jnp.where(sq > 0.0, sq * lax.rsqrt(sq), 0.0)
            a_ref[pl.ds(r0, tc), :] = a
            b_ref[pl.ds(r0, tc), :] = gain * ((ti + 1.0) * half_x)
        return 0
    lax.fori_loop(0, n_chunks, gates, 0)

    n_seq_chunks = n_groups * SUBLANES

    def strided(g, t):
        return pl.ds(g * SUBLANES * pitch + t, SUBLANES, stride=pitch)

    def sweep_totals(t, carry):
        tb = pitch - 1 - t
        out = []
        for g, (pf, hf, pb, hb) in enumerate(carry):
            af = a_f[strided(g, t), :]
            ab = a_b[strided(g, tb), :]
            out.append((pf * af, af * hf + b_f[strided(g, t), :],
                        pb * ab, ab * hb + b_b[strided(g, tb), :]))
        return tuple(out)

    ones = jnp.ones((SUBLANES, LANES), F32)
    zeros = jnp.zeros((SUBLANES, LANES), F32)
    totals = lax.fori_loop(0, pitch, sweep_totals, ((ones, zeros, ones, zeros),) * n_groups,
                           unroll=4)

    def chunk_row(field, k):
        s = k % SUBLANES
        return totals[k // SUBLANES][field][s:s + 1, :]
    cf = [jnp.zeros((1, LANES), F32)]
    for k in range(1, n_seq_chunks):
        cf.append(chunk_row(0, k - 1) * cf[-1] + chunk_row(1, k - 1))
    cbk = [jnp.zeros((1, LANES), F32)]
    for k in range(n_seq_chunks - 2, -1, -1):
        cbk.append(chunk_row(2, k + 1) * cbk[-1] + chunk_row(3, k + 1))
    cbk = cbk[::-1]
    entry = tuple((jnp.concatenate(cf[g * SUBLANES:(g + 1) * SUBLANES], axis=0),
                   jnp.concatenate(cbk[g * SUBLANES:(g + 1) * SUBLANES], axis=0))
                  for g in range(n_groups))

    def sweep_states(t, carry):
        tb = pitch - 1 - t
        out = []
        for g, (hf, hb) in enumerate(carry):
            hf = a_f[strided(g, t), :] * hf + b_f[strided(g, t), :]
            hb = a_b[strided(g, tb), :] * hb + b_b[strided(g, tb), :]
            h_f[strided(g, t), :] = hf
            h_b[strided(g, tb), :] = hb
            out.append((hf, hb))
        return tuple(out)
    lax.fori_loop(0, pitch, sweep_states, entry, unroll=4)

    def emit(c, _):
        r0 = pl.multiple_of(c * tc, tc)
        h = h_f[pl.ds(r0, tc), :] + h_b[pl.ds(r0, tc), :]
        g = g_ref[0, pl.ds(r0, tc), :].astype(F32)
        y_ref[0, pl.ds(r0, tc), :] = (h * _silu(g)).astype(BF16)
        return 0
    lax.fori_loop(0, n_chunks, emit, 0)


def _rg_lru(zl3, conv_w, conv_b, w_r, b_r, w_i, b_i, lam, layer):
    batch, seq_len, lru2 = zl3.shape
    lru_w = lru2 // 2
    n_blk = lru_w // LRU_BLOCK_DIM
    n_seq_chunks = SCAN_GROUPS * SUBLANES
    assert seq_len % LRU_TIME_TILE == 0 and seq_len % n_seq_chunks == 0
    pitch = seq_len // n_seq_chunks + SCAN_PITCH_PAD
    rows = n_seq_chunks * pitch

    col = lambda r: pl.BlockSpec((r, LANES), lambda b, j: (0, j))
    seq = lambda off: pl.BlockSpec((1, seq_len, LANES), lambda b, j: (b, 0, j + off))
    wblk = pl.BlockSpec((None, 2, 1, LRU_BLOCK_DIM, LRU_BLOCK_DIM),
                        lambda b, j: (layer, 0, j, 0, 0))
    return pl.pallas_call(
        functools.partial(_lru_kernel, seq_len=seq_len, pitch=pitch, n_groups=SCAN_GROUPS,
                          tc=LRU_TIME_TILE),
        grid=(batch, n_blk),
        in_specs=[seq(0), seq(n_blk), col(CONV_WIDTH), col(1), wblk, col(2), wblk, col(2), col(2)],
        out_specs=pl.BlockSpec((1, seq_len, LANES), lambda b, j: (b, 0, j)),
        out_shape=jax.ShapeDtypeStruct((batch, seq_len, lru_w), BF16),
        scratch_shapes=[pltpu.VMEM((seq_len + 2 * SUBLANES, LANES), F32)]
                       + [pltpu.VMEM((rows, LANES), F32)] * 6,
        compiler_params=_compiler_params(("parallel", "parallel")),
        name="rg_lru",
    )(zl3, zl3, conv_w, conv_b.reshape(1, lru_w), w_r, b_r, w_i, b_i, lam)


def _attn_kernel(q_ref, k_ref, v_ref, g_ref, o_ref, s_a, s_b, p_a, p_b, m_ref, l_ref, acc_ref,
                 *, tk, rb):
    tq = q_ref.shape[1]
    seq_len = k_ref.shape[1]
    rows = GQA_GROUP * tq
    n_chunks = seq_len // tk
    q = q_ref[0]
    qs = jnp.concatenate([q[:, i * HEAD_DIM:(i + 1) * HEAD_DIM] for i in range(GQA_GROUP)], axis=0)

    def scores(c, s_ref):
        r0 = pl.multiple_of(c * tk, tk)
        kc = k_ref[0, pl.ds(r0, tk), :]
        s_ref[...] = lax.dot_general(qs, kc, (((1,), (1,)), ((), ())), preferred_element_type=F32)

    def softmax(s_ref, p_ref):
        for r in range(rows // rb):
            rs = slice(r * rb, (r + 1) * rb)
            s = [s_ref[rs, j * LANES:(j + 1) * LANES] for j in range(tk // LANES)]
            mx = functools.reduce(jnp.maximum, s)
            m_old = m_ref[rs, :]
            m_new = jnp.maximum(m_old, jnp.max(mx, axis=-1, keepdims=True))
            alpha = jnp.exp2(m_old - m_new)
            p = [jnp.exp2(sj - m_new) for sj in s]
            l_ref[rs, :] = alpha * l_ref[rs, :] + functools.reduce(jnp.add, p)
            for j, pj in enumerate(p):
                p_ref[rs, j * LANES:(j + 1) * LANES] = pj.astype(BF16)
            acc_ref[rs, :] = alpha * acc_ref[rs, :]
            m_ref[rs, :] = m_new

    def weighted_values(c, p_ref):
        r0 = pl.multiple_of(c * tk, tk)
        vc = v_ref[0, pl.ds(r0, tk), :]
        acc_ref[...] += jnp.dot(p_ref[...], vc, preferred_element_type=F32)

    m_ref[...] = jnp.full(m_ref.shape, -jnp.inf, F32)
    l_ref[...] = jnp.zeros(l_ref.shape, F32)
    acc_ref[...] = jnp.zeros(acc_ref.shape, F32)
    scores(0, s_a)
    scores(1, s_b)
    softmax(s_a, p_a)

    def body(i, _):
        c = 2 * i
        weighted_values(c, p_a)
        scores(c + 2, s_a)
        softmax(s_b, p_b)
        weighted_values(c + 1, p_b)
        scores(c + 3, s_b)
        softmax(s_a, p_a)
        return 0
    lax.fori_loop(0, n_chunks // 2 - 1, body, 0)

    weighted_values(n_chunks - 2, p_a)
    softmax(s_b, p_b)
    weighted_values(n_chunks - 1, p_b)

    o = acc_ref[...] / jnp.sum(l_ref[...], axis=-1, keepdims=True)
    o = jnp.concatenate([o[i * tq:(i + 1) * tq, :] for i in range(GQA_GROUP)], axis=1)
    o_ref[0] = (o * _silu(g_ref[0].astype(F32))).astype(BF16)


def _attention(q3, k3, v3, g3):
    batch, seq_len, attn_w = q3.shape
    tq, tk = Q_TILE, KV_TILE
    assert seq_len % tq == 0 and seq_len % (2 * tk) == 0 and seq_len >= 4 * tk
    gw = GQA_GROUP * HEAD_DIM
    rows = GQA_GROUP * tq
    qspec = pl.BlockSpec((1, tq, gw), lambda b, h, i: (b, i, h))
    kvspec = pl.BlockSpec((1, seq_len, HEAD_DIM), lambda b, h, i: (b, 0, h))
    return pl.pallas_call(
        functools.partial(_attn_kernel, tk=tk, rb=ATTN_ROW_BLOCK),
        grid=(batch, N_KV_HEADS, seq_len // tq),
        in_specs=[qspec, kvspec, kvspec, qspec],
        out_specs=qspec,
        out_shape=jax.ShapeDtypeStruct((batch, seq_len, attn_w), BF16),
        scratch_shapes=[pltpu.VMEM((rows, tk), F32)] * 2 + [pltpu.VMEM((rows, tk), BF16)] * 2
                       + [pltpu.VMEM((rows, LANES), F32)] * 2
                       + [pltpu.VMEM((rows, HEAD_DIM), F32)],
        compiler_params=_compiler_params(("parallel", "parallel", "parallel")),
        name="attention",
    )(q3, k3, v3, g3)


def _outproj_kernel(x_ref, zp_ref, prev_ref, next_ref, yl_ref, ya_ref, pw_ref, ps_ref, wo_ref,
                    o_ref, ext_ref, yp_ref, y_ref, *, seq_len, pool_w):
    tm = x_ref.shape[0]
    tiles_per_seq = seq_len // tm
    t0 = (pl.program_id(0) % tiles_per_seq) * tm
    has_prev = (t0 > 0).astype(F32)
    has_next = (t0 + tm < seq_len).astype(F32)
    halo = BF16_ROWS

    lru_w = yl_ref.shape[1]
    y_ref[:, 0:lru_w] = yl_ref[...]
    y_ref[:, lru_w:] = ya_ref[...]
    acc = jnp.dot(y_ref[...], wo_ref[pool_w:, :], preferred_element_type=F32)

    ext_ref[0:halo, :] = prev_ref[...].astype(F32) * has_prev
    ext_ref[halo:halo + tm, :] = zp_ref[:, 0:pool_w].astype(F32)
    ext_ref[halo + tm:, :] = next_ref[...].astype(F32) * has_next

    rc = POOL_ROW_CHUNK
    for c in range(tm // rc):
        t = t0 + c * rc + lax.broadcasted_iota(jnp.int32, (rc, 1), 0)
        for g, w in enumerate(POOL_WINDOWS):
            half = w // 2
            sl = slice(g * LANES, (g + 1) * LANES)
            ext = ext_ref[c * rc:c * rc + rc + 2 * halo, sl]
            u = ext[halo:halo + rc, :]
            span = 1
            while span < w:
                ext = ext + pltpu.roll(ext, span, 0)
                span *= 2
            if half > 1:
                ext = pltpu.roll(ext, rc + 2 * halo - (half - 1), 0)
            win = ext[halo:halo + rc, :]
            cnt = jnp.minimum(t + half, seq_len) - jnp.maximum(t - half, 0)
            pooled = win * (1.0 / cnt.astype(F32)) - u
            mixed = jnp.dot(pooled.astype(BF16), pw_ref[g], preferred_element_type=F32)
            gate = zp_ref[c * rc:(c + 1) * rc, pool_w + g * LANES:pool_w + (g + 1) * LANES].astype(F32)
            yp_ref[c * rc:(c + 1) * rc, sl] = (mixed * ps_ref[:, sl] * _silu(gate)).astype(BF16)

    acc = acc + jnp.dot(yp_ref[...], wo_ref[0:pool_w, :], preferred_element_type=F32)
    o_ref[...] = x_ref[...] + acc


def _outproj(x2, zp, y_lru, y_attn, pool_wt, pool_scale, w_out, layer, seq_len):
    n_rows, d_model = x2.shape
    d_mix = w_out.shape[1]
    pool_w = d_mix // 4
    lru_w = y_lru.shape[1]
    attn_w = y_attn.shape[1]
    tm = ROW_TILE
    halo_blocks = tm // BF16_ROWS
    last_halo = n_rows // BF16_ROWS - 1

    row = lambda w: pl.BlockSpec((tm, w), lambda i: (i, 0))
    const = lambda shape: pl.BlockSpec(shape, lambda i: (0,) * len(shape))
    prev = pl.BlockSpec((BF16_ROWS, pool_w), lambda i: (jnp.maximum(i * halo_blocks - 1, 0), 0))
    nxt = pl.BlockSpec((BF16_ROWS, pool_w),
                       lambda i: (jnp.minimum((i + 1) * halo_blocks, last_halo), 0))
    return pl.pallas_call(
        functools.partial(_outproj_kernel, seq_len=seq_len, pool_w=pool_w),
        grid=(n_rows // tm,),
        in_specs=[row(d_model), row(2 * pool_w), prev, nxt, row(lru_w), row(attn_w),
                  pl.BlockSpec((None,) + pool_wt.shape[1:], lambda i: (layer, 0, 0, 0)),
                  const((1, pool_w)),
                  pl.BlockSpec((None, d_mix, d_model), lambda i: (layer, 0, 0))],
        out_specs=row(d_model),
        out_shape=jax.ShapeDtypeStruct((n_rows, d_model), F32),
        scratch_shapes=[pltpu.VMEM((tm + 2 * BF16_ROWS, pool_w), F32),
                        pltpu.VMEM((tm, pool_w), BF16),
                        pltpu.VMEM((tm, d_mix - pool_w), BF16)],
        compiler_params=_compiler_params(("parallel",)),
        name="outproj",
    )(x2, zp, zp, zp, y_lru, y_attn, pool_wt, pool_scale.reshape(1, pool_w), w_out)


def _rope_tables(seq_len):
    t = np.arange(seq_len)
    inv = ROPE_BASE ** (-np.arange(0, ROPE_AXIS_DIM, 2, dtype=np.float64) / ROPE_AXIS_DIM)
    ang_r = (t // GRID_W)[:, None] * inv
    ang_c = (t % GRID_W)[:, None] * inv
    cos_t = np.concatenate([np.cos(ang_r)] * 2 + [np.cos(ang_c)] * 2, axis=1)
    sin_t = np.concatenate([-np.sin(ang_r), np.sin(ang_r), -np.sin(ang_c), np.sin(ang_c)], axis=1)
    return jnp.asarray(cos_t, F32), jnp.asarray(sin_t, F32)


def kernel(x, norm_g, w_in, pool_w, pool_scale, conv_w, conv_b, lru_wr, lru_br, lru_wi, lru_bi,
           lru_lam, q_norm, k_norm, w_out):
    batch, seq_len, d_model = x.shape
    depth = w_in.shape[0]
    n_rows = batch * seq_len
    cos_t, sin_t = _rope_tables(seq_len)
    w_in_b = w_in.astype(BF16)
    w_out_b = w_out.astype(BF16)
    pool_w_b = pool_w.astype(BF16)
    wr_b = (0.5 * lru_wr).astype(BF16)
    wi_b = (0.5 * lru_wi).astype(BF16)

    x2 = x.reshape(n_rows, d_model)
    for l in range(depth):
        zp, zl, q, k, v, ga = _inproj(x2, norm_g[l], w_in_b, l, q_norm[l], k_norm[l], cos_t, sin_t,
                                      seq_len)
        seq3 = lambda a: a.reshape(batch, seq_len, a.shape[-1])
        y_lru = _rg_lru(seq3(zl), conv_w[l], conv_b[l], wr_b, lru_br[l], wi_b, lru_bi[l],
                        lru_lam[l], l)
        y_attn = _attention(seq3(q), seq3(k), seq3(v), seq3(ga))
        x2 = _outproj(x2, zp, y_lru.reshape(n_rows, -1), y_attn.reshape(n_rows, -1),
                      pool_w_b, pool_scale[l], w_out_b, l, seq_len)
    return x2.reshape(batch, seq_len, d_model)
```

```python
import functools

import jax
import jax.numpy as jnp
import numpy as np
from jax import lax
from jax.experimental import pallas as pl
from jax.experimental.pallas import tpu as pltpu

F32 = jnp.float32
BF16 = jnp.bfloat16

GRID_W = 64
POOL_WINDOWS = (2, 4, 8, 16)
HEAD_DIM = 128
N_Q_HEADS = 6
N_KV_HEADS = 2
GQA_GROUP = N_Q_HEADS // N_KV_HEADS
LRU_BLOCK_DIM = 128
LRU_C = 8.0
CONV_WIDTH = 4
CONV_PAD_LEFT = 1
ROPE_AXIS_DIM = HEAD_DIM // 2
ROPE_BASE = 10000.0
EPS = 1e-6
Q_SCALE = HEAD_DIM ** -0.5 * 1.4426950408889634

LANES = 128
SUBLANES = 8
BF16_ROWS = 16
VMEM_LIMIT_BYTES = 56 * 1024 * 1024

ROW_TILE = 512
Q_TILE = 512
KV_TILE = 512
ATTN_ROW_BLOCK = 32
POOL_ROW_CHUNK = 128
LRU_TIME_TILE = 1024
SCAN_GROUPS = 4
SCAN_PITCH_PAD = 4


def _silu(x):
    h = 0.5 * x
    return h * jnp.tanh(h) + h


def _compiler_params(semantics):
    return pltpu.CompilerParams(dimension_semantics=semantics, vmem_limit_bytes=VMEM_LIMIT_BYTES)


def _inproj_kernel(x_ref, g_ref, w_ref, qn_ref, kn_ref, cos_ref, sin_ref,
                   zp_ref, zl_ref, q_ref, k_ref, v_ref, ga_ref, *, widths):
    pool2, lru2, qw, kw, vw, gw = widths
    x = x_ref[...]
    ms = jnp.mean(x * x, axis=-1, keepdims=True)
    h = (x * lax.rsqrt(ms + EPS) * g_ref[...]).astype(BF16)

    def proj(lo, width):
        return jnp.dot(h, w_ref[:, lo:lo + width], preferred_element_type=F32)

    cos = cos_ref[...]
    sin = sin_ref[...]
    lane = lax.broadcasted_iota(jnp.int32, cos.shape, 1)
    low_half = (lane & (ROPE_AXIS_DIM // 2)) == 0

    def norm_rope(zh, gain, scale):
        msh = jnp.mean(zh * zh, axis=-1, keepdims=True)
        y = zh * lax.rsqrt(msh + EPS) * gain
        partner = jnp.where(low_half,
                            pltpu.roll(y, HEAD_DIM - ROPE_AXIS_DIM // 2, 1),
                            pltpu.roll(y, ROPE_AXIS_DIM // 2, 1))
        return (y * cos + partner * sin) * scale

    q_off = pool2 + lru2
    zq = proj(q_off, qw)
    for hd in range(qw // HEAD_DIM):
        sl = slice(hd * HEAD_DIM, (hd + 1) * HEAD_DIM)
        q_ref[:, sl] = norm_rope(zq[:, sl], qn_ref[...], Q_SCALE).astype(BF16)
    zk = proj(q_off + qw, kw)
    for hd in range(kw // HEAD_DIM):
        sl = slice(hd * HEAD_DIM, (hd + 1) * HEAD_DIM)
        k_ref[:, sl] = norm_rope(zk[:, sl], kn_ref[...], 1.0).astype(BF16)
    v_ref[...] = proj(q_off + qw + kw, vw).astype(BF16)
    ga_ref[...] = proj(q_off + qw + kw + vw, gw).astype(BF16)
    zp_ref[...] = proj(0, pool2).astype(BF16)
    zl_ref[...] = proj(pool2, lru2).astype(BF16)


def _inproj(x2, norm_g, w_in, layer, q_norm, k_norm, cos_t, sin_t, seq_len):
    n_rows, d_model = x2.shape
    d_in = w_in.shape[-1]
    d_mix = 2 * d_model
    pool_w, lru_w, attn_w = d_mix // 4, 3 * d_mix // 8, 3 * d_mix // 8
    kv_w = N_KV_HEADS * HEAD_DIM
    widths = (2 * pool_w, 2 * lru_w, attn_w, kv_w, kv_w, attn_w)
    assert sum(widths) == d_in
    tm = ROW_TILE
    assert n_rows % tm == 0 and seq_len % tm == 0
    tiles_per_seq = seq_len // tm

    row = lambda w: pl.BlockSpec((tm, w), lambda i: (i, 0))
    const = lambda shape: pl.BlockSpec(shape, lambda i: (0,) * len(shape))
    pos = pl.BlockSpec((tm, HEAD_DIM), lambda i: (i % tiles_per_seq, 0))
    out_shape = tuple(jax.ShapeDtypeStruct((n_rows, w), BF16) for w in widths)
    return pl.pallas_call(
        functools.partial(_inproj_kernel, widths=widths),
        grid=(n_rows // tm,),
        in_specs=[row(d_model), const((1, d_model)),
                  pl.BlockSpec((None, d_model, d_in), lambda i: (layer, 0, 0)),
                  const((1, HEAD_DIM)), const((1, HEAD_DIM)), pos, pos],
        out_specs=tuple(row(w) for w in widths),
        out_shape=out_shape,
        compiler_params=_compiler_params(("parallel",)),
        name="inproj",
    )(x2, norm_g.reshape(1, d_model), w_in, q_norm.reshape(1, HEAD_DIM),
      k_norm.reshape(1, HEAD_DIM), cos_t, sin_t)


def _lru_kernel(u_ref, g_ref, cw_ref, cb_ref, wr_ref, br_ref, wi_ref, bi_ref, lam_ref, y_ref,
                ubuf, a_f, b_f, a_b, b_b, h_f, h_b, *, seq_len, pitch, n_groups, tc):
    n_chunks = seq_len // tc
    pad_rows = n_groups * SUBLANES * pitch - seq_len
    halo = SUBLANES

    ubuf[0:halo, :] = jnp.zeros((halo, LANES), F32)
    ubuf[halo + seq_len:2 * halo + seq_len, :] = jnp.zeros((halo, LANES), F32)

    def stage(c, _):
        r0 = pl.multiple_of(c * tc, tc)
        ubuf[pl.ds(halo + r0, tc), :] = u_ref[0, pl.ds(r0, tc), :].astype(F32)
        return 0
    lax.fori_loop(0, n_chunks, stage, 0)

    for a_ref, b_ref in ((a_f, b_f), (a_b, b_b)):
        a_ref[seq_len:seq_len + pad_rows, :] = jnp.ones((pad_rows, LANES), F32)
        b_ref[seq_len:seq_len + pad_rows, :] = jnp.zeros((pad_rows, LANES), F32)

    lam = lam_ref[...]
    log_sig_lam = jnp.minimum(lam, 0.0) - jnp.log1p(jnp.exp(-jnp.abs(lam)))
    half_c = (0.5 * LRU_C) * log_sig_lam
    cw = cw_ref[...]
    cb = cb_ref[...]
    br_half = 0.5 * br_ref[...]
    bi_half = 0.5 * bi_ref[...]

    def gates(c, _):
        r0 = pl.multiple_of(c * tc, tc)
        xc = cb
        for k in range(CONV_WIDTH):
            xk = ubuf[pl.ds(halo + r0 + (k - CONV_PAD_LEFT), tc), :]
            xc = xc + xk * cw[k:k + 1, :]
        xcb = xc.astype(BF16)
        half_x = 0.5 * xc
        for d, (a_ref, b_ref) in enumerate(((a_f, b_f), (a_b, b_b))):
            tr = jnp.tanh(jnp.dot(xcb, wr_ref[d, 0], preferred_element_type=F32) + br_half[d:d + 1, :])
            ti = jnp.tanh(jnp.dot(xcb, wi_ref[d, 0], preferred_element_type=F32) + bi_half[d:d + 1, :])
            log_a = tr * half_c[d:d + 1, :] + half_c[d:d + 1, :]
            a = jnp.exp(log_a)
            sq = jnp.tanh(log_a) * (-1.0 - a * a)
            gain = jnp.where(sq > 0.0, sq * lax.rsqrt(sq), 0.0)
            a_ref[pl.ds(r0, tc), :] = a
            b_ref[pl.ds(r0, tc), :] = gain * ((ti + 1.0) * half_x)
        return 0
    lax.fori_loop(0, n_chunks, gates, 0)

    n_seq_chunks = n_groups * SUBLANES

    def strided(g, t):
        return pl.ds(g * SUBLANES * pitch + t, SUBLANES, stride=pitch)

    def sweep_totals(t, carry):
        tb = pitch - 1 - t
        out = []
        for g, (pf, hf, pb, hb) in enumerate(carry):
            af = a_f[strided(g, t), :]
            ab = a_b[strided(g, tb), :]
            out.append((pf * af, af * hf + b_f[strided(g, t), :],
                        pb * ab, ab * hb + b_b[strided(g, tb), :]))
        return tuple(out)

    ones = jnp.ones((SUBLANES, LANES), F32)
    zeros = jnp.zeros((SUBLANES, LANES), F32)
    totals = lax.fori_loop(0, pitch, sweep_totals, ((ones, zeros, ones, zeros),) * n_groups,
                           unroll=4)

    def chunk_row(field, k):
        s = k % SUBLANES
        return totals[k // SUBLANES][field][s:s + 1, :]
    cf = [jnp.zeros((1, LANES), F32)]
    for k in range(1, n_seq_chunks):
        cf.append(chunk_row(0, k - 1) * cf[-1] + chunk_row(1, k - 1))
    cbk = [jnp.zeros((1, LANES), F32)]
    for k in range(n_seq_chunks - 2, -1, -1):
        cbk.append(chunk_row(2, k + 1) * cbk[-1] + chunk_row(3, k + 1))
    cbk = cbk[::-1]
    entry = tuple((jnp.concatenate(cf[g * SUBLANES:(g + 1) * SUBLANES], axis=0),
                   jnp.concatenate(cbk[g * SUBLANES:(g + 1) * SUBLANES], axis=0))
                  for g in range(n_groups))

    def sweep_states(t, carry):
        tb = pitch - 1 - t
        out = []
        for g, (hf, hb) in enumerate(carry):
            hf = a_f[strided(g, t), :] * hf + b_f[strided(g, t), :]
            hb = a_b[strided(g, tb), :] * hb + b_b[strided(g, tb), :]
            h_f[strided(g, t), :] = hf
            h_b[strided(g, tb), :] = hb
            out.append((hf, hb))
        return tuple(out)
    lax.fori_loop(0, pitch, sweep_states, entry, unroll=4)

    def emit(c, _):
        r0 = pl.multiple_of(c * tc, tc)
        h = h_f[pl.ds(r0, tc), :] + h_b[pl.ds(r0, tc), :]
        g = g_ref[0, pl.ds(r0, tc), :].astype(F32)
        y_ref[0, pl.ds(r0, tc), :] = (h * _silu(g)).astype(BF16)
        return 0
    lax.fori_loop(0, n_chunks, emit, 0)


def _rg_lru(zl3, conv_w, conv_b, w_r, b_r, w_i, b_i, lam, layer):
    batch, seq_len, lru2 = zl3.shape
    lru_w = lru2 // 2
    n_blk = lru_w // LRU_BLOCK_DIM
    n_seq_chunks = SCAN_GROUPS * SUBLANES
    assert seq_len % LRU_TIME_TILE == 0 and seq_len % n_seq_chunks == 0
    pitch = seq_len // n_seq_chunks + SCAN_PITCH_PAD
    rows = n_seq_chunks * pitch

    col = lambda r: pl.BlockSpec((r, LANES), lambda b, j: (0, j))
    seq = lambda off: pl.BlockSpec((1, seq_len, LANES), lambda b, j: (b, 0, j + off))
    wblk = pl.BlockSpec((None, 2, 1, LRU_BLOCK_DIM, LRU_BLOCK_DIM),
                        lambda b, j: (layer, 0, j, 0, 0))
    return pl.pallas_call(
        functools.partial(_lru_kernel, seq_len=seq_len, pitch=pitch, n_groups=SCAN_GROUPS,
                          tc=LRU_TIME_TILE),
        grid=(batch, n_blk),
        in_specs=[seq(0), seq(n_blk), col(CONV_WIDTH), col(1), wblk, col(2), wblk, col(2), col(2)],
        out_specs=pl.BlockSpec((1, seq_len, LANES), lambda b, j: (b, 0, j)),
        out_shape=jax.ShapeDtypeStruct((batch, seq_len, lru_w), BF16),
        scratch_shapes=[pltpu.VMEM((seq_len + 2 * SUBLANES, LANES), F32)]
                       + [pltpu.VMEM((rows, LANES), F32)] * 6,
        compiler_params=_compiler_params(("parallel", "parallel")),
        name="rg_lru",
    )(zl3, zl3, conv_w, conv_b.reshape(1, lru_w), w_r, b_r, w_i, b_i, lam)


def _attn_kernel(q_ref, k_ref, v_ref, g_ref, o_ref, s_a, s_b, p_a, p_b, m_ref, l_ref, acc_ref,
                 *, tk, rb):
    tq = q_ref.shape[1]
    seq_len = k_ref.shape[1]
    rows = GQA_GROUP * tq
    n_chunks = seq_len // tk
    q = q_ref[0]
    qs = jnp.concatenate([q[:, i * HEAD_DIM:(i + 1) * HEAD_DIM] for i in range(GQA_GROUP)], axis=0)

    def scores(c, s_ref):
        r0 = pl.multiple_of(c * tk, tk)
        kc = k_ref[0, pl.ds(r0, tk), :]
        s_ref[...] = lax.dot_general(qs, kc, (((1,), (1,)), ((), ())), preferred_element_type=F32)

    def softmax(s_ref, p_ref):
        for r in range(rows // rb):
            rs = slice(r * rb, (r + 1) * rb)
            s = [s_ref[rs, j * LANES:(j + 1) * LANES] for j in range(tk // LANES)]
            mx = functools.reduce(jnp.maximum, s)
            m_old = m_ref[rs, :]
            m_new = jnp.maximum(m_old, jnp.max(mx, axis=-1, keepdims=True))
            alpha = jnp.exp2(m_old - m_new)
            p = [jnp.exp2(sj - m_new) for sj in s]
            l_ref[rs, :] = alpha * l_ref[rs, :] + functools.reduce(jnp.add, p)
            for j, pj in enumerate(p):
                p_ref[rs, j * LANES:(j + 1) * LANES] = pj.astype(BF16)
            acc_ref[rs, :] = alpha * acc_ref[rs, :]
            m_ref[rs, :] = m_new

    def weighted_values(c, p_ref):
        r0 = pl.multiple_of(c * tk, tk)
        vc = v_ref[0, pl.ds(r0, tk), :]
        acc_ref[...] += jnp.dot(p_ref[...], vc, preferred_element_type=F32)

    m_ref[...] = jnp.full(m_ref.shape, -jnp.inf, F32)
    l_ref[...] = jnp.zeros(l_ref.shape, F32)
    acc_ref[...] = jnp.zeros(acc_ref.shape, F32)
    scores(0, s_a)
    scores(1, s_b)
    softmax(s_a, p_a)

    def pipeline_pair(c):
        weighted_values(c, p_a)
        scores(c + 2, s_a)
        softmax(s_b, p_b)
        weighted_values(c + 1, p_b)
        scores(c + 3, s_b)
        softmax(s_a, p_a)

    n_pairs = n_chunks // 2 - 1

    def body(i, _):
        for k in range(2):
            pipeline_pair(4 * i + 2 * k)
        return 0
    lax.fori_loop(0, n_pairs // 2, body, 0)
    if n_pairs % 2:
        pipeline_pair(2 * (n_pairs - 1))

    weighted_values(n_chunks - 2, p_a)
    softmax(s_b, p_b)
    weighted_values(n_chunks - 1, p_b)

    o = acc_ref[...] / jnp.sum(l_ref[...], axis=-1, keepdims=True)
    o = jnp.concatenate([o[i * tq:(i + 1) * tq, :] for i in range(GQA_GROUP)], axis=1)
    o_ref[0] = (o * _silu(g_ref[0].astype(F32))).astype(BF16)


def _attention(q3, k3, v3, g3):
    batch, seq_len, attn_w = q3.shape
    tq, tk = Q_TILE, KV_TILE
    assert seq_len % tq == 0 and seq_len % (2 * tk) == 0 and seq_len >= 4 * tk
    gw = GQA_GROUP * HEAD_DIM
    rows = GQA_GROUP * tq
    qspec = pl.BlockSpec((1, tq, gw), lambda b, h, i: (b, i, h))
    kvspec = pl.BlockSpec((1, seq_len, HEAD_DIM), lambda b, h, i: (b, 0, h))
    return pl.pallas_call(
        functools.partial(_attn_kernel, tk=tk, rb=ATTN_ROW_BLOCK),
        grid=(batch, N_KV_HEADS, seq_len // tq),
        in_specs=[qspec, kvspec, kvspec, qspec],
        out_specs=qspec,
        out_shape=jax.ShapeDtypeStruct((batch, seq_len, attn_w), BF16),
        scratch_shapes=[pltpu.VMEM((rows, tk), F32)] * 2 + [pltpu.VMEM((rows, tk), BF16)] * 2
                       + [pltpu.VMEM((rows, LANES), F32)] * 2
                       + [pltpu.VMEM((rows, HEAD_DIM), F32)],
        compiler_params=_compiler_params(("parallel", "parallel", "parallel")),
        name="attention",
    )(q3, k3, v3, g3)


def _outproj_kernel(x_ref, zp_ref, prev_ref, next_ref, yl_ref, ya_ref, pw_ref, ps_ref, wo_ref,
                    o_ref, ext_ref, yp_ref, y_ref, *, seq_len, pool_w):
    tm = x_ref.shape[0]
    tiles_per_seq = seq_len // tm
    t0 = (pl.program_id(0) % tiles_per_seq) * tm
    has_prev = (t0 > 0).astype(F32)
    has_next = (t0 + tm < seq_len).astype(F32)
    halo = BF16_ROWS

    lru_w = yl_ref.shape[1]
    y_ref[:, 0:lru_w] = yl_ref[...]
    y_ref[:, lru_w:] = ya_ref[...]
    acc = jnp.dot(y_ref[...], wo_ref[pool_w:, :], preferred_element_type=F32)

    ext_ref[0:halo, :] = prev_ref[...].astype(F32) * has_prev
    ext_ref[halo:halo + tm, :] = zp_ref[:, 0:pool_w].astype(F32)
    ext_ref[halo + tm:, :] = next_ref[...].astype(F32) * has_next

    rc = POOL_ROW_CHUNK
    for c in range(tm // rc):
        t = t0 + c * rc + lax.broadcasted_iota(jnp.int32, (rc, 1), 0)
        for g, w in enumerate(POOL_WINDOWS):
            half = w // 2
            sl = slice(g * LANES, (g + 1) * LANES)
            ext = ext_ref[c * rc:c * rc + rc + 2 * halo, sl]
            u = ext[halo:halo + rc, :]
            span = 1
            while span < w:
                ext = ext + pltpu.roll(ext, span, 0)
                span *= 2
            if half > 1:
                ext = pltpu.roll(ext, rc + 2 * halo - (half - 1), 0)
            win = ext[halo:halo + rc, :]
            cnt = jnp.minimum(t + half, seq_len) - jnp.maximum(t - half, 0)
            pooled = win * (1.0 / cnt.astype(F32)) - u
            mixed = jnp.dot(pooled.astype(BF16), pw_ref[g], preferred_element_type=F32)
            gate = zp_ref[c * rc:(c + 1) * rc, pool_w + g * LANES:pool_w + (g + 1) * LANES].astype(F32)
            yp_ref[c * rc:(c + 1) * rc, sl] = (mixed * ps_ref[:, sl] * _silu(gate)).astype(BF16)

    acc = acc + jnp.dot(yp_ref[...], wo_ref[0:pool_w, :], preferred_element_type=F32)
    o_ref[...] = x_ref[...] + acc


def _outproj(x2, zp, y_lru, y_attn, pool_wt, pool_scale, w_out, layer, seq_len):
    n_rows, d_model = x2.shape
    d_mix = w_out.shape[1]
    pool_w = d_mix // 4
    lru_w = y_lru.shape[1]
    attn_w = y_attn.shape[1]
    tm = ROW_TILE
    halo_blocks = tm // BF16_ROWS
    last_halo = n_rows // BF16_ROWS - 1

    row = lambda w: pl.BlockSpec((tm, w), lambda i: (i, 0))
    const = lambda shape: pl.BlockSpec(shape, lambda i: (0,) * len(shape))
    prev = pl.BlockSpec((BF16_ROWS, pool_w), lambda i: (jnp.maximum(i * halo_blocks - 1, 0), 0))
    nxt = pl.BlockSpec((BF16_ROWS, pool_w),
                       lambda i: (jnp.minimum((i + 1) * halo_blocks, last_halo), 0))
    return pl.pallas_call(
        functools.partial(_outproj_kernel, seq_len=seq_len, pool_w=pool_w),
        grid=(n_rows // tm,),
        in_specs=[row(d_model), row(2 * pool_w), prev, nxt, row(lru_w), row(attn_w),
                  pl.BlockSpec((None,) + pool_wt.shape[1:], lambda i: (layer, 0, 0, 0)),
                  const((1, pool_w)),
                  pl.BlockSpec((None, d_mix, d_model), lambda i: (layer, 0, 0))],
        out_specs=row(d_model),
        out_shape=jax.ShapeDtypeStruct((n_rows, d_model), F32),
        scratch_shapes=[pltpu.VMEM((tm + 2 * BF16_ROWS, pool_w), F32),
                        pltpu.VMEM((tm, pool_w), BF16),
                        pltpu.VMEM((tm, d_mix - pool_w), BF16)],
        compiler_params=_compiler_params(("parallel",)),
        name="outproj",
    )(x2, zp, zp, zp, y_lru, y_attn, pool_wt, pool_scale.reshape(1, pool_w), w_out)


def _rope_tables(seq_len):
    t = np.arange(seq_len)
    inv = ROPE_BASE ** (-np.arange(0, ROPE_AXIS_DIM, 2, dtype=np.float64) / ROPE_AXIS_DIM)
    ang_r = (t // GRID_W)[:, None] * inv
    ang_c = (t % GRID_W)[:, None] * inv
    cos_t = np.concatenate([np.cos(ang_r)] * 2 + [np.cos(ang_c)] * 2, axis=1)
    sin_t = np.concatenate([-np.sin(ang_r), np.sin(ang_r), -np.sin(ang_c), np.sin(ang_c)], axis=1)
    return jnp.asarray(cos_t, F32), jnp.asarray(sin_t, F32)


def kernel(x, norm_g, w_in, pool_w, pool_scale, conv_w, conv_b, lru_wr, lru_br, lru_wi, lru_bi,
           lru_lam, q_norm, k_norm, w_out):
    batch, seq_len, d_model = x.shape
    depth = w_in.shape[0]
    n_rows = batch * seq_len
    cos_t, sin_t = _rope_tables(seq_len)
    w_in_b = w_in.astype(BF16)
    w_out_b = w_out.astype(BF16)
    pool_w_b = pool_w.astype(BF16)
    wr_b = (0.5 * lru_wr).astype(BF16)
    wi_b = (0.5 * lru_wi).astype(BF16)

    x2 = x.reshape(n_rows, d_model)
    for l in range(depth):
        zp, zl, q, k, v, ga = _inproj(x2, norm_g[l], w_in_b, l, q_norm[l], k_norm[l], cos_t, sin_t,
                                      seq_len)
        seq3 = lambda a: a.reshape(batch, seq_len, a.shape[-1])
        y_lru = _rg_lru(seq3(zl), conv_w[l], conv_b[l], wr_b, lru_br[l], wi_b, lru_bi[l],
                        lru_lam[l], l)
        y_attn = _attention(seq3(q), seq3(k), seq3(v), seq3(ga))
        x2 = _outproj(x2, zp, y_lru.reshape(n_rows, -1), y_attn.reshape(n_rows, -1),
                      pool_w_b, pool_scale[l], w_out_b, l, seq_len)
    return x2.reshape(batch, seq_len, d_model)
```

```python
import functools

import jax
import jax.numpy as jnp
import numpy as np
from jax import lax
from jax.experimental import pallas as pl
from jax.experimental.pallas import tpu as pltpu

F32 = jnp.float32
BF16 = jnp.bfloat16

GRID_W = 64
POOL_WINDOWS = (2, 4, 8, 16)
HEAD_DIM = 128
N_Q_HEADS = 6
N_KV_HEADS = 2
GQA_GROUP = N_Q_HEADS // N_KV_HEADS
LRU_BLOCK_DIM = 128
LRU_C = 8.0
CONV_WIDTH = 4
CONV_PAD_LEFT = 1
ROPE_AXIS_DIM = HEAD_DIM // 2
ROPE_BASE = 10000.0
EPS = 1e-6
Q_SCALE = HEAD_DIM ** -0.5 * 1.4426950408889634

LANES = 128
SUBLANES = 8
BF16_ROWS = 16
VMEM_LIMIT_BYTES = 56 * 1024 * 1024

ROW_TILE = 512
Q_TILE = 512
KV_TILE = 512
ATTN_ROW_BLOCK = 32
POOL_ROW_CHUNK = 128
LRU_TIME_TILE = 1024
SCAN_GROUPS = 4
SCAN_PITCH_PAD = 4
SCAN_MAX_UNROLL = 10


def _silu(x):
    h = 0.5 * x
    return h * jnp.tanh(h) + h


def _compiler_params(semantics):
    return pltpu.CompilerParams(dimension_semantics=semantics, vmem_limit_bytes=VMEM_LIMIT_BYTES)


def _inproj_kernel(x_ref, g_ref, w_ref, qn_ref, kn_ref, cos_ref, sin_ref,
                   zp_ref, zl_ref, q_ref, k_ref, v_ref, ga_ref, *, widths):
    pool2, lru2, qw, kw, vw, gw = widths
    x = x_ref[...]
    ms = jnp.mean(x * x, axis=-1, keepdims=True)
    h = (x * lax.rsqrt(ms + EPS) * g_ref[...]).astype(BF16)

    def proj(lo, width):
        return jnp.dot(h, w_ref[:, lo:lo + width], preferred_element_type=F32)

    cos = cos_ref[...]
    sin = sin_ref[...]
    lane = lax.broadcasted_iota(jnp.int32, cos.shape, 1)
    low_half = (lane & (ROPE_AXIS_DIM // 2)) == 0

    def norm_rope(zh, gain, scale):
        msh = jnp.mean(zh * zh, axis=-1, keepdims=True)
        y = zh * lax.rsqrt(msh + EPS) * gain
        partner = jnp.where(low_half,
                            pltpu.roll(y, HEAD_DIM - ROPE_AXIS_DIM // 2, 1),
                            pltpu.roll(y, ROPE_AXIS_DIM // 2, 1))
        return (y * cos + partner * sin) * scale

    q_off = pool2 + lru2
    zq = proj(q_off, qw)
    for hd in range(qw // HEAD_DIM):
        sl = slice(hd * HEAD_DIM, (hd + 1) * HEAD_DIM)
        q_ref[:, sl] = norm_rope(zq[:, sl], qn_ref[...], Q_SCALE).astype(BF16)
    zk = proj(q_off + qw, kw)
    for hd in range(kw // HEAD_DIM):
        sl = slice(hd * HEAD_DIM, (hd + 1) * HEAD_DIM)
        k_ref[:, sl] = norm_rope(zk[:, sl], kn_ref[...], 1.0).astype(BF16)
    v_ref[...] = proj(q_off + qw + kw, vw).astype(BF16)
    ga_ref[...] = proj(q_off + qw + kw + vw, gw).astype(BF16)
    zp_ref[...] = proj(0, pool2).astype(BF16)
    zl_ref[...] = proj(pool2, lru2).astype(BF16)


def _inproj(x2, norm_g, w_in, layer, q_norm, k_norm, cos_t, sin_t, seq_len):
    n_rows, d_model = x2.shape
    d_in = w_in.shape[-1]
    d_mix = 2 * d_model
    pool_w, lru_w, attn_w = d_mix // 4, 3 * d_mix // 8, 3 * d_mix // 8
    kv_w = N_KV_HEADS * HEAD_DIM
    widths = (2 * pool_w, 2 * lru_w, attn_w, kv_w, kv_w, attn_w)
    assert sum(widths) == d_in
    tm = ROW_TILE
    assert n_rows % tm == 0 and seq_len % tm == 0
    tiles_per_seq = seq_len // tm

    row = lambda w: pl.BlockSpec((tm, w), lambda i: (i, 0))
    const = lambda shape: pl.BlockSpec(shape, lambda i: (0,) * len(shape))
    pos = pl.BlockSpec((tm, HEAD_DIM), lambda i: (i % tiles_per_seq, 0))
    out_shape = tuple(jax.ShapeDtypeStruct((n_rows, w), BF16) for w in widths)
    return pl.pallas_call(
        functools.partial(_inproj_kernel, widths=widths),
        grid=(n_rows // tm,),
        in_specs=[row(d_model), const((1, d_model)),
                  pl.BlockSpec((None, d_model, d_in), lambda i: (layer, 0, 0)),
                  const((1, HEAD_DIM)), const((1, HEAD_DIM)), pos, pos],
        out_specs=tuple(row(w) for w in widths),
        out_shape=out_shape,
        compiler_params=_compiler_params(("parallel",)),
        name="inproj",
    )(x2, norm_g.reshape(1, d_model), w_in, q_norm.reshape(1, HEAD_DIM),
      k_norm.reshape(1, HEAD_DIM), cos_t, sin_t)


def _lru_kernel(u_ref, g_ref, cw_ref, cb_ref, wr_ref, br_ref, wi_ref, bi_ref, lam_ref, y_ref,
                ubuf, a_f, b_f, a_b, b_b, h_f, h_b, *, seq_len, pitch, n_groups, tc, unroll):
    n_chunks = seq_len // tc
    pad_rows = n_groups * SUBLANES * pitch - seq_len
    halo = SUBLANES

    ubuf[0:halo, :] = jnp.zeros((halo, LANES), F32)
    ubuf[halo + seq_len:2 * halo + seq_len, :] = jnp.zeros((halo, LANES), F32)

    def stage(c, _):
        r0 = pl.multiple_of(c * tc, tc)
        ubuf[pl.ds(halo + r0, tc), :] = u_ref[0, pl.ds(r0, tc), :].astype(F32)
        return 0
    lax.fori_loop(0, n_chunks, stage, 0)

    for a_ref, b_ref in ((a_f, b_f), (a_b, b_b)):
        a_ref[seq_len:seq_len + pad_rows, :] = jnp.ones((pad_rows, LANES), F32)
        b_ref[seq_len:seq_len + pad_rows, :] = jnp.zeros((pad_rows, LANES), F32)

    lam = lam_ref[...]
    log_sig_lam = jnp.minimum(lam, 0.0) - jnp.log1p(jnp.exp(-jnp.abs(lam)))
    half_c = (0.5 * LRU_C) * log_sig_lam
    cw = cw_ref[...]
    cb = cb_ref[...]
    br_half = 0.5 * br_ref[...]
    bi_half = 0.5 * bi_ref[...]

    def gates(c, _):
        r0 = pl.multiple_of(c * tc, tc)
        xc = cb
        for k in range(CONV_WIDTH):
            xk = ubuf[pl.ds(halo + r0 + (k - CONV_PAD_LEFT), tc), :]
            xc = xc + xk * cw[k:k + 1, :]
        xcb = xc.astype(BF16)
        half_x = 0.5 * xc
        for d, (a_ref, b_ref) in enumerate(((a_f, b_f), (a_b, b_b))):
            tr = jnp.tanh(jnp.dot(xcb, wr_ref[d, 0], preferred_element_type=F32) + br_half[d:d + 1, :])
            ti = jnp.tanh(jnp.dot(xcb, wi_ref[d, 0], preferred_element_type=F32) + bi_half[d:d + 1, :])
            log_a = tr * half_c[d:d + 1, :] + half_c[d:d + 1, :]
            a = jnp.exp(log_a)
            sq = jnp.tanh(log_a) * (-1.0 - a * a)
            gain = jnp.where(sq > 0.0, sq * lax.rsqrt(sq), 0.0)
            a_ref[pl.ds(r0, tc), :] = a
            b_ref[pl.ds(r0, tc), :] = gain * ((ti + 1.0) * half_x)
        return 0
    lax.fori_loop(0, n_chunks, gates, 0)

    n_seq_chunks = n_groups * SUBLANES

    def strided(g, t):
        return pl.ds(g * SUBLANES * pitch + t, SUBLANES, stride=pitch)

    def sweep_totals(t, carry):
        tb = pitch - 1 - t
        out = []
        for g, (pf, hf, pb, hb) in enumerate(carry):
            af = a_f[strided(g, t), :]
            ab = a_b[strided(g, tb), :]
            out.append((pf * af, af * hf + b_f[strided(g, t), :],
                        pb * ab, ab * hb + b_b[strided(g, tb), :]))
        return tuple(out)

    ones = jnp.ones((SUBLANES, LANES), F32)
    zeros = jnp.zeros((SUBLANES, LANES), F32)
    totals = lax.fori_loop(0, pitch, sweep_totals, ((ones, zeros, ones, zeros),) * n_groups,
                           unroll=unroll)

    def chunk_row(field, k):
        s = k % SUBLANES
        return totals[k // SUBLANES][field][s:s + 1, :]
    cf = [jnp.zeros((1, LANES), F32)]
    for k in range(1, n_seq_chunks):
        cf.append(chunk_row(0, k - 1) * cf[-1] + chunk_row(1, k - 1))
    cbk = [jnp.zeros((1, LANES), F32)]
    for k in range(n_seq_chunks - 2, -1, -1):
        cbk.append(chunk_row(2, k + 1) * cbk[-1] + chunk_row(3, k + 1))
    cbk = cbk[::-1]
    entry = tuple((jnp.concatenate(cf[g * SUBLANES:(g + 1) * SUBLANES], axis=0),
                   jnp.concatenate(cbk[g * SUBLANES:(g + 1) * SUBLANES], axis=0))
                  for g in range(n_groups))

    def sweep_states(t, carry):
        tb = pitch - 1 - t
        out = []
        for g, (hf, hb) in enumerate(carry):
            hf = a_f[strided(g, t), :] * hf + b_f[strided(g, t), :]
            hb = a_b[strided(g, tb), :] * hb + b_b[strided(g, tb), :]
            h_f[strided(g, t), :] = hf
            h_b[strided(g, tb), :] = hb
            out.append((hf, hb))
        return tuple(out)
    lax.fori_loop(0, pitch, sweep_states, entry, unroll=unroll)

    def emit(c, _):
        r0 = pl.multiple_of(c * tc, tc)
        h = h_f[pl.ds(r0, tc), :] + h_b[pl.ds(r0, tc), :]
        g = g_ref[0, pl.ds(r0, tc), :].astype(F32)
        y_ref[0, pl.ds(r0, tc), :] = (h * _silu(g)).astype(BF16)
        return 0
    lax.fori_loop(0, n_chunks, emit, 0)


def _rg_lru(zl3, conv_w, conv_b, w_r, b_r, w_i, b_i, lam, layer):
    batch, seq_len, lru2 = zl3.shape
    lru_w = lru2 // 2
    n_blk = lru_w // LRU_BLOCK_DIM
    n_seq_chunks = SCAN_GROUPS * SUBLANES
    assert seq_len % LRU_TIME_TILE == 0 and seq_len % n_seq_chunks == 0
    pitch = seq_len // n_seq_chunks + SCAN_PITCH_PAD
    rows = n_seq_chunks * pitch
    unroll = max(d for d in range(1, SCAN_MAX_UNROLL + 1) if pitch % d == 0)

    col = lambda r: pl.BlockSpec((r, LANES), lambda b, j: (0, j))
    seq = lambda off: pl.BlockSpec((1, seq_len, LANES), lambda b, j: (b, 0, j + off))
    wblk = pl.BlockSpec((None, 2, 1, LRU_BLOCK_DIM, LRU_BLOCK_DIM),
                        lambda b, j: (layer, 0, j, 0, 0))
    return pl.pallas_call(
        functools.partial(_lru_kernel, seq_len=seq_len, pitch=pitch, n_groups=SCAN_GROUPS,
                          tc=LRU_TIME_TILE, unroll=unroll),
        grid=(batch, n_blk),
        in_specs=[seq(0), seq(n_blk), col(CONV_WIDTH), col(1), wblk, col(2), wblk, col(2), col(2)],
        out_specs=pl.BlockSpec((1, seq_len, LANES), lambda b, j: (b, 0, j)),
        out_shape=jax.ShapeDtypeStruct((batch, seq_len, lru_w), BF16),
        scratch_shapes=[pltpu.VMEM((seq_len + 2 * SUBLANES, LANES), F32)]
                       + [pltpu.VMEM((rows, LANES), F32)] * 6,
        compiler_params=_compiler_params(("parallel", "parallel")),
        name="rg_lru",
    )(zl3, zl3, conv_w, conv_b.reshape(1, lru_w), w_r, b_r, w_i, b_i, lam)


def _attn_kernel(q_ref, k_ref, v_ref, g_ref, o_ref, s_a, s_b, p_a, p_b, m_ref, l_ref, acc_ref,
                 *, tk, rb):
    tq = q_ref.shape[1]
    seq_len = k_ref.shape[1]
    rows = GQA_GROUP * tq
    n_chunks = seq_len // tk
    q = q_ref[0]
    qs = jnp.concatenate([q[:, i * HEAD_DIM:(i + 1) * HEAD_DIM] for i in range(GQA_GROUP)], axis=0)

    def scores(c, s_ref):
        r0 = pl.multiple_of(c * tk, tk)
        kc = k_ref[0, pl.ds(r0, tk), :]
        s_ref[...] = lax.dot_general(qs, kc, (((1,), (1,)), ((), ())), preferred_element_type=F32)

    def softmax(s_ref, p_ref):
        for r in range(rows // rb):
            rs = slice(r * rb, (r + 1) * rb)
            s = [s_ref[rs, j * LANES:(j + 1) * LANES] for j in range(tk // LANES)]
            mx = functools.reduce(jnp.maximum, s)
            m_old = m_ref[rs, :]
            m_new = jnp.maximum(m_old, jnp.max(mx, axis=-1, keepdims=True))
            alpha = jnp.exp2(m_old - m_new)
            p = [jnp.exp2(sj - m_new) for sj in s]
            l_ref[rs, :] = alpha * l_ref[rs, :] + functools.reduce(jnp.add, p)
            for j, pj in enumerate(p):
                p_ref[rs, j * LANES:(j + 1) * LANES] = pj.astype(BF16)
            acc_ref[rs, :] = alpha * acc_ref[rs, :]
            m_ref[rs, :] = m_new

    def weighted_values(c, p_ref):
        r0 = pl.multiple_of(c * tk, tk)
        vc = v_ref[0, pl.ds(r0, tk), :]
        acc_ref[...] += jnp.dot(p_ref[...], vc, preferred_element_type=F32)

    m_ref[...] = jnp.full(m_ref.shape, -jnp.inf, F32)
    l_ref[...] = jnp.zeros(l_ref.shape, F32)
    acc_ref[...] = jnp.zeros(acc_ref.shape, F32)
    scores(0, s_a)
    scores(1, s_b)
    softmax(s_a, p_a)

    def pipeline_pair(c):
        weighted_values(c, p_a)
        scores(c + 2, s_a)
        softmax(s_b, p_b)
        weighted_values(c + 1, p_b)
        scores(c + 3, s_b)
        softmax(s_a, p_a)

    for pair in range(n_chunks // 2 - 1):
        pipeline_pair(2 * pair)

    weighted_values(n_chunks - 2, p_a)
    softmax(s_b, p_b)
    weighted_values(n_chunks - 1, p_b)

    o = acc_ref[...] / jnp.sum(l_ref[...], axis=-1, keepdims=True)
    o = jnp.concatenate([o[i * tq:(i + 1) * tq, :] for i in range(GQA_GROUP)], axis=1)
    o_ref[0] = (o * _silu(g_ref[0].astype(F32))).astype(BF16)


def _attention(q3, k3, v3, g3):
    batch, seq_len, attn_w = q3.shape
    tq, tk = Q_TILE, KV_TILE
    assert seq_len % tq == 0 and seq_len % (2 * tk) == 0 and seq_len >= 4 * tk
    gw = GQA_GROUP * HEAD_DIM
    rows = GQA_GROUP * tq
    qspec = pl.BlockSpec((1, tq, gw), lambda b, h, i: (b, i, h))
    kvspec = pl.BlockSpec((1, seq_len, HEAD_DIM), lambda b, h, i: (b, 0, h))
    return pl.pallas_call(
        functools.partial(_attn_kernel, tk=tk, rb=ATTN_ROW_BLOCK),
        grid=(batch, N_KV_HEADS, seq_len // tq),
        in_specs=[qspec, kvspec, kvspec, qspec],
        out_specs=qspec,
        out_shape=jax.ShapeDtypeStruct((batch, seq_len, attn_w), BF16),
        scratch_shapes=[pltpu.VMEM((rows, tk), F32)] * 2 + [pltpu.VMEM((rows, tk), BF16)] * 2
                       + [pltpu.VMEM((rows, LANES), F32)] * 2
                       + [pltpu.VMEM((rows, HEAD_DIM), F32)],
        compiler_params=_compiler_params(("parallel", "parallel", "parallel")),
        name="attention",
    )(q3, k3, v3, g3)


def _outproj_kernel(x_ref, zp_ref, prev_ref, next_ref, yl_ref, ya_ref, pw_ref, ps_ref, wo_ref,
                    o_ref, ext_ref, yp_ref, y_ref, *, seq_len, pool_w):
    tm = x_ref.shape[0]
    tiles_per_seq = seq_len // tm
    t0 = (pl.program_id(0) % tiles_per_seq) * tm
    has_prev = (t0 > 0).astype(F32)
    has_next = (t0 + tm < seq_len).astype(F32)
    halo = BF16_ROWS

    lru_w = yl_ref.shape[1]
    y_ref[:, 0:lru_w] = yl_ref[...]
    y_ref[:, lru_w:] = ya_ref[...]
    acc = jnp.dot(y_ref[...], wo_ref[pool_w:, :], preferred_element_type=F32)

    ext_ref[0:halo, :] = prev_ref[...].astype(F32) * has_prev
    ext_ref[halo:halo + tm, :] = zp_ref[:, 0:pool_w].astype(F32)
    ext_ref[halo + tm:, :] = next_ref[...].astype(F32) * has_next

    rc = POOL_ROW_CHUNK
    for c in range(tm // rc):
        t = t0 + c * rc + lax.broadcasted_iota(jnp.int32, (rc, 1), 0)
        for g, w in enumerate(POOL_WINDOWS):
            half = w // 2
            sl = slice(g * LANES, (g + 1) * LANES)
            ext = ext_ref[c * rc:c * rc + rc + 2 * halo, sl]
            u = ext[halo:halo + rc, :]
            span = 1
            while span < w:
                ext = ext + pltpu.roll(ext, span, 0)
                span *= 2
            if half > 1:
                ext = pltpu.roll(ext, rc + 2 * halo - (half - 1), 0)
            win = ext[halo:halo + rc, :]
            cnt = jnp.minimum(t + half, seq_len) - jnp.maximum(t - half, 0)
            pooled = win * (1.0 / cnt.astype(F32)) - u
            mixed = jnp.dot(pooled.astype(BF16), pw_ref[g], preferred_element_type=F32)
            gate = zp_ref[c * rc:(c + 1) * rc, pool_w + g * LANES:pool_w + (g + 1) * LANES].astype(F32)
            yp_ref[c * rc:(c + 1) * rc, sl] = (mixed * ps_ref[:, sl] * _silu(gate)).astype(BF16)

    acc = acc + jnp.dot(yp_ref[...], wo_ref[0:pool_w, :], preferred_element_type=F32)
    o_ref[...] = x_ref[...] + acc


def _outproj(x2, zp, y_lru, y_attn, pool_wt, pool_scale, w_out, layer, seq_len):
    n_rows, d_model = x2.shape
    d_mix = w_out.shape[1]
    pool_w = d_mix // 4
    lru_w = y_lru.shape[1]
    attn_w = y_attn.shape[1]
    tm = ROW_TILE
    halo_blocks = tm // BF16_ROWS
    last_halo = n_rows // BF16_ROWS - 1

    row = lambda w: pl.BlockSpec((tm, w), lambda i: (i, 0))
    const = lambda shape: pl.BlockSpec(shape, lambda i: (0,) * len(shape))
    prev = pl.BlockSpec((BF16_ROWS, pool_w), lambda i: (jnp.maximum(i * halo_blocks - 1, 0), 0))
    nxt = pl.BlockSpec((BF16_ROWS, pool_w),
                       lambda i: (jnp.minimum((i + 1) * halo_blocks, last_halo), 0))
    return pl.pallas_call(
        functools.partial(_outproj_kernel, seq_len=seq_len, pool_w=pool_w),
        grid=(n_rows // tm,),
        in_specs=[row(d_model), row(2 * pool_w), prev, nxt, row(lru_w), row(attn_w),
                  pl.BlockSpec((None,) + pool_wt.shape[1:], lambda i: (layer, 0, 0, 0)),
                  const((1, pool_w)),
                  pl.BlockSpec((None, d_mix, d_model), lambda i: (layer, 0, 0))],
        out_specs=row(d_model),
        out_shape=jax.ShapeDtypeStruct((n_rows, d_model), F32),
        scratch_shapes=[pltpu.VMEM((tm + 2 * BF16_ROWS, pool_w), F32),
                        pltpu.VMEM((tm, pool_w), BF16),
                        pltpu.VMEM((tm, d_mix - pool_w), BF16)],
        compiler_params=_compiler_params(("parallel",)),
        name="outproj",
    )(x2, zp, zp, zp, y_lru, y_attn, pool_wt, pool_scale.reshape(1, pool_w), w_out)


def _rope_tables(seq_len):
    t = np.arange(seq_len)
    inv = ROPE_BASE ** (-np.arange(0, ROPE_AXIS_DIM, 2, dtype=np.float64) / ROPE_AXIS_DIM)
    ang_r = (t // GRID_W)[:, None] * inv
    ang_c = (t % GRID_W)[:, None] * inv
    cos_t = np.concatenate([np.cos(ang_r)] * 2 + [np.cos(ang_c)] * 2, axis=1)
    sin_t = np.concatenate([-np.sin(ang_r), np.sin(ang_r), -np.sin(ang_c), np.sin(ang_c)], axis=1)
    return jnp.asarray(cos_t, F32), jnp.asarray(sin_t, F32)


def kernel(x, norm_g, w_in, pool_w, pool_scale, conv_w, conv_b, lru_wr, lru_br, lru_wi, lru_bi,
           lru_lam, q_norm, k_norm, w_out):
    batch, seq_len, d_model = x.shape
    depth = w_in.shape[0]
    n_rows = batch * seq_len
    cos_t, sin_t = _rope_tables(seq_len)
    w_in_b = w_in.astype(BF16)
    w_out_b = w_out.astype(BF16)
    pool_w_b = pool_w.astype(BF16)
    wr_b = (0.5 * lru_wr).astype(BF16)
    wi_b = (0.5 * lru_wi).astype(BF16)

    x2 = x.reshape(n_rows, d_model)
    for l in range(depth):
        zp, zl, q, k, v, ga = _inproj(x2, norm_g[l], w_in_b, l, q_norm[l], k_norm[l], cos_t, sin_t,
                                      seq_len)
        seq3 = lambda a: a.reshape(batch, seq_len, a.shape[-1])
        y_lru = _rg_lru(seq3(zl), conv_w[l], conv_b[l], wr_b, lru_br[l], wi_b, lru_bi[l],
                        lru_lam[l], l)
        y_attn = _attention(seq3(q), seq3(k), seq3(v), seq3(ga))
        x2 = _outproj(x2, zp, y_lru.reshape(n_rows, -1), y_attn.reshape(n_rows, -1),
                      pool_w_b, pool_scale[l], w_out_b, l, seq_len)
    return x2.reshape(batch, seq_len, d_model)
```

```python
import functools

import jax
import jax.numpy as jnp
import numpy as np
from jax import lax
from jax.experimental import pallas as pl
from jax.experimental.pallas import tpu as pltpu

F32 = jnp.float32
BF16 = jnp.bfloat16

GRID_W = 64
POOL_WINDOWS = (2, 4, 8, 16)
HEAD_DIM = 128
N_Q_HEADS = 6
N_KV_HEADS = 2
GQA_GROUP = N_Q_HEADS // N_KV_HEADS
LRU_BLOCK_DIM = 128
LRU_C = 8.0
CONV_WIDTH = 4
CONV_PAD_LEFT = 1
ROPE_AXIS_DIM = HEAD_DIM // 2
ROPE_BASE = 10000.0
EPS = 1e-6
Q_SCALE = HEAD_DIM ** -0.5 * 1.4426950408889634

LANES = 128
SUBLANES = 8
BF16_ROWS = 16
VMEM_LIMIT_BYTES = 56 * 1024 * 1024

ROW_TILE = 512
KV_TILE = 512
ATTN_ROW_BLOCK = 32
ATTN_LAYER_VARIANTS = (
    dict(tq=512, pairs_per_trip=2, mxu_row_sums=False),
    dict(tq=512, pairs_per_trip=3, mxu_row_sums=False),
    dict(tq=1024, pairs_per_trip=2, mxu_row_sums=False),
    dict(tq=512, pairs_per_trip=2, mxu_row_sums=True),
)
POOL_ROW_CHUNK = 128
LRU_TIME_TILE = 1024
SCAN_GROUPS = 4
SCAN_PITCH_PAD = 4
SCAN_MAX_UNROLL = 10


def _silu(x):
    h = 0.5 * x
    return h * jnp.tanh(h) + h


def _compiler_params(semantics):
    return pltpu.CompilerParams(dimension_semantics=semantics, vmem_limit_bytes=VMEM_LIMIT_BYTES)


def _inproj_kernel(x_ref, g_ref, w_ref, qn_ref, kn_ref, cos_ref, sin_ref,
                   zp_ref, zl_ref, q_ref, k_ref, v_ref, ga_ref, *, widths):
    pool2, lru2, qw, kw, vw, gw = widths
    x = x_ref[...]
    ms = jnp.mean(x * x, axis=-1, keepdims=True)
    h = (x * lax.rsqrt(ms + EPS) * g_ref[...]).astype(BF16)

    def proj(lo, width):
        return jnp.dot(h, w_ref[:, lo:lo + width], preferred_element_type=F32)

    cos = cos_ref[...]
    sin = sin_ref[...]
    lane = lax.broadcasted_iota(jnp.int32, cos.shape, 1)
    low_half = (lane & (ROPE_AXIS_DIM // 2)) == 0

    def norm_rope(zh, gain, scale):
        msh = jnp.mean(zh * zh, axis=-1, keepdims=True)
        y = zh * lax.rsqrt(msh + EPS) * gain
        partner = jnp.where(low_half,
                            pltpu.roll(y, HEAD_DIM - ROPE_AXIS_DIM // 2, 1),
                            pltpu.roll(y, ROPE_AXIS_DIM // 2, 1))
        return (y * cos + partner * sin) * scale

    q_off = pool2 + lru2
    zq = proj(q_off, qw)
    for hd in range(qw // HEAD_DIM):
        sl = slice(hd * HEAD_DIM, (hd + 1) * HEAD_DIM)
        q_ref[:, sl] = norm_rope(zq[:, sl], qn_ref[...], Q_SCALE).astype(BF16)
    zk = proj(q_off + qw, kw)
    for hd in range(kw // HEAD_DIM):
        sl = slice(hd * HEAD_DIM, (hd + 1) * HEAD_DIM)
        k_ref[:, sl] = norm_rope(zk[:, sl], kn_ref[...], 1.0).astype(BF16)
    v_ref[...] = proj(q_off + qw + kw, vw).astype(BF16)
    ga_ref[...] = proj(q_off + qw + kw + vw, gw).astype(BF16)
    zp_ref[...] = proj(0, pool2).astype(BF16)
    zl_ref[...] = proj(pool2, lru2).astype(BF16)


def _inproj(x2, norm_g, w_in, layer, q_norm, k_norm, cos_t, sin_t, seq_len):
    n_rows, d_model = x2.shape
    d_in = w_in.shape[-1]
    d_mix = 2 * d_model
    pool_w, lru_w, attn_w = d_mix // 4, 3 * d_mix // 8, 3 * d_mix // 8
    kv_w = N_KV_HEADS * HEAD_DIM
    widths = (2 * pool_w, 2 * lru_w, attn_w, kv_w, kv_w, attn_w)
    assert sum(widths) == d_in
    tm = ROW_TILE
    assert n_rows % tm == 0 and seq_len % tm == 0
    tiles_per_seq = seq_len // tm

    row = lambda w: pl.BlockSpec((tm, w), lambda i: (i, 0))
    const = lambda shape: pl.BlockSpec(shape, lambda i: (0,) * len(shape))
    pos = pl.BlockSpec((tm, HEAD_DIM), lambda i: (i % tiles_per_seq, 0))
    out_shape = tuple(jax.ShapeDtypeStruct((n_rows, w), BF16) for w in widths)
    return pl.pallas_call(
        functools.partial(_inproj_kernel, widths=widths),
        grid=(n_rows // tm,),
        in_specs=[row(d_model), const((1, d_model)),
                  pl.BlockSpec((None, d_model, d_in), lambda i: (layer, 0, 0)),
                  const((1, HEAD_DIM)), const((1, HEAD_DIM)), pos, pos],
        out_specs=tuple(row(w) for w in widths),
        out_shape=out_shape,
        compiler_params=_compiler_params(("parallel",)),
        name="inproj",
    )(x2, norm_g.reshape(1, d_model), w_in, q_norm.reshape(1, HEAD_DIM),
      k_norm.reshape(1, HEAD_DIM), cos_t, sin_t)


def _lru_kernel(u_ref, g_ref, cw_ref, cb_ref, wr_ref, br_ref, wi_ref, bi_ref, lam_ref, y_ref,
                ubuf, a_f, b_f, a_b, b_b, h_f, h_b, *, seq_len, pitch, n_groups, tc, unroll):
    n_chunks = seq_len // tc
    pad_rows = n_groups * SUBLANES * pitch - seq_len
    halo = SUBLANES

    ubuf[0:halo, :] = jnp.zeros((halo, LANES), F32)
    ubuf[halo + seq_len:2 * halo + seq_len, :] = jnp.zeros((halo, LANES), F32)

    def stage(c, _):
        r0 = pl.multiple_of(c * tc, tc)
        ubuf[pl.ds(halo + r0, tc), :] = u_ref[0, pl.ds(r0, tc), :].astype(F32)
        return 0
    lax.fori_loop(0, n_chunks, stage, 0)

    for a_ref, b_ref in ((a_f, b_f), (a_b, b_b)):
        a_ref[seq_len:seq_len + pad_rows, :] = jnp.ones((pad_rows, LANES), F32)
        b_ref[seq_len:seq_len + pad_rows, :] = jnp.zeros((pad_rows, LANES), F32)

    lam = lam_ref[...]
    log_sig_lam = jnp.minimum(lam, 0.0) - jnp.log1p(jnp.exp(-jnp.abs(lam)))
    half_c = (0.5 * LRU_C) * log_sig_lam
    cw = cw_ref[...]
    cb = cb_ref[...]
    br_half = 0.5 * br_ref[...]
    bi_half = 0.5 * bi_ref[...]

    def gates(c, _):
        r0 = pl.multiple_of(c * tc, tc)
        xc = cb
        for k in range(CONV_WIDTH):
            xk = ubuf[pl.ds(halo + r0 + (k - CONV_PAD_LEFT), tc), :]
            xc = xc + xk * cw[k:k + 1, :]
        xcb = xc.astype(BF16)
        half_x = 0.5 * xc
        for d, (a_ref, b_ref) in enumerate(((a_f, b_f), (a_b, b_b))):
            tr = jnp.tanh(jnp.dot(xcb, wr_ref[d, 0], preferred_element_type=F32) + br_half[d:d + 1, :])
            ti = jnp.tanh(jnp.dot(xcb, wi_ref[d, 0], preferred_element_type=F32) + bi_half[d:d + 1, :])
            log_a = tr * half_c[d:d + 1, :] + half_c[d:d + 1, :]
            a = jnp.exp(log_a)
            sq = jnp.tanh(log_a) * (-1.0 - a * a)
            gain = jnp.where(sq > 0.0, sq * lax.rsqrt(sq), 0.0)
            a_ref[pl.ds(r0, tc), :] = a
            b_ref[pl.ds(r0, tc), :] = gain * ((ti + 1.0) * half_x)
        return 0
    lax.fori_loop(0, n_chunks, gates, 0)

    n_seq_chunks = n_groups * SUBLANES

    def strided(g, t):
        return pl.ds(g * SUBLANES * pitch + t, SUBLANES, stride=pitch)

    def sweep_totals(t, carry):
        tb = pitch - 1 - t
        out = []
        for g, (pf, hf, pb, hb) in enumerate(carry):
            af = a_f[strided(g, t), :]
            ab = a_b[strided(g, tb), :]
            out.append((pf * af, af * hf + b_f[strided(g, t), :],
                        pb * ab, ab * hb + b_b[strided(g, tb), :]))
        return tuple(out)

    ones = jnp.ones((SUBLANES, LANES), F32)
    zeros = jnp.zeros((SUBLANES, LANES), F32)
    totals = lax.fori_loop(0, pitch, sweep_totals, ((ones, zeros, ones, zeros),) * n_groups,
                           unroll=unroll)

    def chunk_row(field, k):
        s = k % SUBLANES
        return totals[k // SUBLANES][field][s:s + 1, :]
    cf = [jnp.zeros((1, LANES), F32)]
    for k in range(1, n_seq_chunks):
        cf.append(chunk_row(0, k - 1) * cf[-1] + chunk_row(1, k - 1))
    cbk = [jnp.zeros((1, LANES), F32)]
    for k in range(n_seq_chunks - 2, -1, -1):
        cbk.append(chunk_row(2, k + 1) * cbk[-1] + chunk_row(3, k + 1))
    cbk = cbk[::-1]
    entry = tuple((jnp.concatenate(cf[g * SUBLANES:(g + 1) * SUBLANES], axis=0),
                   jnp.concatenate(cbk[g * SUBLANES:(g + 1) * SUBLANES], axis=0))
                  for g in range(n_groups))

    def sweep_states(t, carry):
        tb = pitch - 1 - t
        out = []
        for g, (hf, hb) in enumerate(carry):
            hf = a_f[strided(g, t), :] * hf + b_f[strided(g, t), :]
            hb = a_b[strided(g, tb), :] * hb + b_b[strided(g, tb), :]
            h_f[strided(g, t), :] = hf
            h_b[strided(g, tb), :] = hb
            out.append((hf, hb))
        return tuple(out)
    lax.fori_loop(0, pitch, sweep_states, entry, unroll=unroll)

    def emit(c, _):
        r0 = pl.multiple_of(c * tc, tc)
        h = h_f[pl.ds(r0, tc), :] + h_b[pl.ds(r0, tc), :]
        g = g_ref[0, pl.ds(r0, tc), :].astype(F32)
        y_ref[0, pl.ds(r0, tc), :] = (h * _silu(g)).astype(BF16)
        return 0
    lax.fori_loop(0, n_chunks, emit, 0)


def _rg_lru(zl3, conv_w, conv_b, w_r, b_r, w_i, b_i, lam, layer):
    batch, seq_len, lru2 = zl3.shape
    lru_w = lru2 // 2
    n_blk = lru_w // LRU_BLOCK_DIM
    n_seq_chunks = SCAN_GROUPS * SUBLANES
    assert seq_len % LRU_TIME_TILE == 0 and seq_len % n_seq_chunks == 0
    pitch = seq_len // n_seq_chunks + SCAN_PITCH_PAD
    rows = n_seq_chunks * pitch
    unroll = max(d for d in range(1, SCAN_MAX_UNROLL + 1) if pitch % d == 0)

    col = lambda r: pl.BlockSpec((r, LANES), lambda b, j: (0, j))
    seq = lambda off: pl.BlockSpec((1, seq_len, LANES), lambda b, j: (b, 0, j + off))
    wblk = pl.BlockSpec((None, 2, 1, LRU_BLOCK_DIM, LRU_BLOCK_DIM),
                        lambda b, j: (layer, 0, j, 0, 0))
    return pl.pallas_call(
        functools.partial(_lru_kernel, seq_len=seq_len, pitch=pitch, n_groups=SCAN_GROUPS,
                          tc=LRU_TIME_TILE, unroll=unroll),
        grid=(batch, n_blk),
        in_specs=[seq(0), seq(n_blk), col(CONV_WIDTH), col(1), wblk, col(2), wblk, col(2), col(2)],
        out_specs=pl.BlockSpec((1, seq_len, LANES), lambda b, j: (b, 0, j)),
        out_shape=jax.ShapeDtypeStruct((batch, seq_len, lru_w), BF16),
        scratch_shapes=[pltpu.VMEM((seq_len + 2 * SUBLANES, LANES), F32)]
                       + [pltpu.VMEM((rows, LANES), F32)] * 6,
        compiler_params=_compiler_params(("parallel", "parallel")),
        name="rg_lru",
    )(zl3, zl3, conv_w, conv_b.reshape(1, lru_w), w_r, b_r, w_i, b_i, lam)


def _attn_kernel(q_ref, k_ref, v_ref, g_ref, o_ref, s_a, s_b, p_a, p_b, m_ref, l_ref, acc_ref,
                 *, tk, rb, pairs_per_trip, mxu_row_sums):
    tq = q_ref.shape[1]
    seq_len = k_ref.shape[1]
    rows = GQA_GROUP * tq
    n_chunks = seq_len // tk
    q = q_ref[0]
    qs = jnp.concatenate([q[:, i * HEAD_DIM:(i + 1) * HEAD_DIM] for i in range(GQA_GROUP)], axis=0)

    def scores(c, s_ref):
        r0 = pl.multiple_of(c * tk, tk)
        kc = k_ref[0, pl.ds(r0, tk), :]
        s_ref[...] = lax.dot_general(qs, kc, (((1,), (1,)), ((), ())), preferred_element_type=F32)

    def softmax(s_ref, p_ref):
        for r in range(rows // rb):
            rs = slice(r * rb, (r + 1) * rb)
            s = [s_ref[rs, j * LANES:(j + 1) * LANES] for j in range(tk // LANES)]
            mx = functools.reduce(jnp.maximum, s)
            m_old = m_ref[rs, :]
            m_new = jnp.maximum(m_old, jnp.max(mx, axis=-1, keepdims=True))
            alpha = jnp.exp2(m_old - m_new)
            if mxu_row_sums:
                for j, sj in enumerate(s):
                    p_ref[rs, j * LANES:(j + 1) * LANES] = jnp.exp2((sj - m_new).astype(BF16))
            else:
                p = [jnp.exp2(sj - m_new) for sj in s]
                l_ref[rs, :] = alpha * l_ref[rs, :] + functools.reduce(jnp.add, p)
                for j, pj in enumerate(p):
                    p_ref[rs, j * LANES:(j + 1) * LANES] = pj.astype(BF16)
            for half in range(acc_ref.shape[1] // HEAD_DIM):
                hs = slice(half * HEAD_DIM, (half + 1) * HEAD_DIM)
                acc_ref[rs, hs] = alpha * acc_ref[rs, hs]
            m_ref[rs, :] = m_new

    ones = jnp.ones((tk, LANES), BF16)

    def weighted_values(c, p_ref):
        r0 = pl.multiple_of(c * tk, tk)
        vc = v_ref[0, pl.ds(r0, tk), :]
        if mxu_row_sums:
            vc = jnp.concatenate([vc, ones], axis=1)
        acc_ref[...] += jnp.dot(p_ref[...], vc, preferred_element_type=F32)

    m_ref[...] = jnp.full(m_ref.shape, -jnp.inf, F32)
    l_ref[...] = jnp.zeros(l_ref.shape, F32)
    acc_ref[...] = jnp.zeros(acc_ref.shape, F32)
    scores(0, s_a)
    scores(1, s_b)
    softmax(s_a, p_a)

    def pipeline_pair(c):
        weighted_values(c, p_a)
        scores(c + 2, s_a)
        softmax(s_b, p_b)
        weighted_values(c + 1, p_b)
        scores(c + 3, s_b)
        softmax(s_a, p_a)

    n_pairs = n_chunks // 2 - 1

    def body(i, _):
        for k in range(pairs_per_trip):
            pipeline_pair(2 * (pairs_per_trip * i + k))
        return 0
    lax.fori_loop(0, n_pairs // pairs_per_trip, body, 0)
    for pair in range(n_pairs - n_pairs % pairs_per_trip, n_pairs):
        pipeline_pair(2 * pair)

    weighted_values(n_chunks - 2, p_a)
    softmax(s_b, p_b)
    weighted_values(n_chunks - 1, p_b)

    if mxu_row_sums:
        o = acc_ref[:, 0:HEAD_DIM] / acc_ref[:, HEAD_DIM:]
    else:
        o = acc_ref[...] / jnp.sum(l_ref[...], axis=-1, keepdims=True)
    o = jnp.concatenate([o[i * tq:(i + 1) * tq, :] for i in range(GQA_GROUP)], axis=1)
    o_ref[0] = (o * _silu(g_ref[0].astype(F32))).astype(BF16)


def _attention(q3, k3, v3, g3, *, tq, pairs_per_trip, mxu_row_sums):
    batch, seq_len, attn_w = q3.shape
    tk = KV_TILE
    assert seq_len % tq == 0 and seq_len % (2 * tk) == 0 and seq_len >= 4 * tk
    gw = GQA_GROUP * HEAD_DIM
    rows = GQA_GROUP * tq
    acc_w = 2 * HEAD_DIM if mxu_row_sums else HEAD_DIM
    qspec = pl.BlockSpec((1, tq, gw), lambda b, h, i: (b, i, h))
    kvspec = pl.BlockSpec((1, seq_len, HEAD_DIM), lambda b, h, i: (b, 0, h))
    return pl.pallas_call(
        functools.partial(_attn_kernel, tk=tk, rb=ATTN_ROW_BLOCK, pairs_per_trip=pairs_per_trip,
                          mxu_row_sums=mxu_row_sums),
        grid=(batch, N_KV_HEADS, seq_len // tq),
        in_specs=[qspec, kvspec, kvspec, qspec],
        out_specs=qspec,
        out_shape=jax.ShapeDtypeStruct((batch, seq_len, attn_w), BF16),
        scratch_shapes=[pltpu.VMEM((rows, tk), F32)] * 2 + [pltpu.VMEM((rows, tk), BF16)] * 2
                       + [pltpu.VMEM((rows, LANES), F32)] * 2
                       + [pltpu.VMEM((rows, acc_w), F32)],
        compiler_params=_compiler_params(("parallel", "parallel", "parallel")),
        name="attention",
    )(q3, k3, v3, g3)


def _outproj_kernel(x_ref, zp_ref, prev_ref, next_ref, yl_ref, ya_ref, pw_ref, ps_ref, wo_ref,
                    o_ref, ext_ref, yp_ref, y_ref, *, seq_len, pool_w):
    tm = x_ref.shape[0]
    tiles_per_seq = seq_len // tm
    t0 = (pl.program_id(0) % tiles_per_seq) * tm
    has_prev = (t0 > 0).astype(F32)
    has_next = (t0 + tm < seq_len).astype(F32)
    halo = BF16_ROWS

    lru_w = yl_ref.shape[1]
    y_ref[:, 0:lru_w] = yl_ref[...]
    y_ref[:, lru_w:] = ya_ref[...]
    acc = jnp.dot(y_ref[...], wo_ref[pool_w:, :], preferred_element_type=F32)

    ext_ref[0:halo, :] = prev_ref[...].astype(F32) * has_prev
    ext_ref[halo:halo + tm, :] = zp_ref[:, 0:pool_w].astype(F32)
    ext_ref[halo + tm:, :] = next_ref[...].astype(F32) * has_next

    rc = POOL_ROW_CHUNK
    for c in range(tm // rc):
        t = t0 + c * rc + lax.broadcasted_iota(jnp.int32, (rc, 1), 0)
        for g, w in enumerate(POOL_WINDOWS):
            half = w // 2
            sl = slice(g * LANES, (g + 1) * LANES)
            ext = ext_ref[c * rc:c * rc + rc + 2 * halo, sl]
            u = ext[halo:halo + rc, :]
            span = 1
            while span < w:
                ext = ext + pltpu.roll(ext, span, 0)
                span *= 2
            if half > 1:
                ext = pltpu.roll(ext, rc + 2 * halo - (half - 1), 0)
            win = ext[halo:halo + rc, :]
            cnt = jnp.minimum(t + half, seq_len) - jnp.maximum(t - half, 0)
            pooled = win * (1.0 / cnt.astype(F32)) - u
            mixed = jnp.dot(pooled.astype(BF16), pw_ref[g], preferred_element_type=F32)
            gate = zp_ref[c * rc:(c + 1) * rc, pool_w + g * LANES:pool_w + (g + 1) * LANES].astype(F32)
            yp_ref[c * rc:(c + 1) * rc, sl] = (mixed * ps_ref[:, sl] * _silu(gate)).astype(BF16)

    acc = acc + jnp.dot(yp_ref[...], wo_ref[0:pool_w, :], preferred_element_type=F32)
    o_ref[...] = x_ref[...] + acc


def _outproj(x2, zp, y_lru, y_attn, pool_wt, pool_scale, w_out, layer, seq_len):
    n_rows, d_model = x2.shape
    d_mix = w_out.shape[1]
    pool_w = d_mix // 4
    lru_w = y_lru.shape[1]
    attn_w = y_attn.shape[1]
    tm = ROW_TILE
    halo_blocks = tm // BF16_ROWS
    last_halo = n_rows // BF16_ROWS - 1

    row = lambda w: pl.BlockSpec((tm, w), lambda i: (i, 0))
    const = lambda shape: pl.BlockSpec(shape, lambda i: (0,) * len(shape))
    prev = pl.BlockSpec((BF16_ROWS, pool_w), lambda i: (jnp.maximum(i * halo_blocks - 1, 0), 0))
    nxt = pl.BlockSpec((BF16_ROWS, pool_w),
                       lambda i: (jnp.minimum((i + 1) * halo_blocks, last_halo), 0))
    return pl.pallas_call(
        functools.partial(_outproj_kernel, seq_len=seq_len, pool_w=pool_w),
        grid=(n_rows // tm,),
        in_specs=[row(d_model), row(2 * pool_w), prev, nxt, row(lru_w), row(attn_w),
                  pl.BlockSpec((None,) + pool_wt.shape[1:], lambda i: (layer, 0, 0, 0)),
                  const((1, pool_w)),
                  pl.BlockSpec((None, d_mix, d_model), lambda i: (layer, 0, 0))],
        out_specs=row(d_model),
        out_shape=jax.ShapeDtypeStruct((n_rows, d_model), F32),
        scratch_shapes=[pltpu.VMEM((tm + 2 * BF16_ROWS, pool_w), F32),
                        pltpu.VMEM((tm, pool_w), BF16),
                        pltpu.VMEM((tm, d_mix - pool_w), BF16)],
        compiler_params=_compiler_params(("parallel",)),
        name="outproj",
    )(x2, zp, zp, zp, y_lru, y_attn, pool_wt, pool_scale.reshape(1, pool_w), w_out)


def _rope_tables(seq_len):
    t = np.arange(seq_len)
    inv = ROPE_BASE ** (-np.arange(0, ROPE_AXIS_DIM, 2, dtype=np.float64) / ROPE_AXIS_DIM)
    ang_r = (t // GRID_W)[:, None] * inv
    ang_c = (t % GRID_W)[:, None] * inv
    cos_t = np.concatenate([np.cos(ang_r)] * 2 + [np.cos(ang_c)] * 2, axis=1)
    sin_t = np.concatenate([-np.sin(ang_r), np.sin(ang_r), -np.sin(ang_c), np.sin(ang_c)], axis=1)
    return jnp.asarray(cos_t, F32), jnp.asarray(sin_t, F32)


def kernel(x, norm_g, w_in, pool_w, pool_scale, conv_w, conv_b, lru_wr, lru_br, lru_wi, lru_bi,
           lru_lam, q_norm, k_norm, w_out):
    batch, seq_len, d_model = x.shape
    depth = w_in.shape[0]
    n_rows = batch * seq_len
    cos_t, sin_t = _rope_tables(seq_len)
    w_in_b = w_in.astype(BF16)
    w_out_b = w_out.astype(BF16)
    pool_w_b = pool_w.astype(BF16)
    wr_b = (0.5 * lru_wr).astype(BF16)
    wi_b = (0.5 * lru_wi).astype(BF16)

    x2 = x.reshape(n_rows, d_model)
    for l in range(depth):
        zp, zl, q, k, v, ga = _inproj(x2, norm_g[l], w_in_b, l, q_norm[l], k_norm[l], cos_t, sin_t,
                                      seq_len)
        seq3 = lambda a: a.reshape(batch, seq_len, a.shape[-1])
        y_lru = _rg_lru(seq3(zl), conv_w[l], conv_b[l], wr_b, lru_br[l], wi_b, lru_bi[l],
                        lru_lam[l], l)
        y_attn = _attention(seq3(q), seq3(k), seq3(v), seq3(ga), **ATTN_LAYER_VARIANTS[l % 4])
        x2 = _outproj(x2, zp, y_lru.reshape(n_rows, -1), y_attn.reshape(n_rows, -1),
                      pool_w_b, pool_scale[l], w_out_b, l, seq_len)
    return x2.reshape(batch, seq_len, d_model)
```

```python
import functools

import jax
import jax.numpy as jnp
import numpy as np
from jax import lax
from jax.experimental import pallas as pl
from jax.experimental.pallas import tpu as pltpu

F32 = jnp.float32
BF16 = jnp.bfloat16

GRID_W = 64
POOL_WINDOWS = (2, 4, 8, 16)
HEAD_DIM = 128
N_Q_HEADS = 6
N_KV_HEADS = 2
GQA_GROUP = N_Q_HEADS // N_KV_HEADS
LRU_BLOCK_DIM = 128
LRU_C = 8.0
CONV_WIDTH = 4
CONV_PAD_LEFT = 1
ROPE_AXIS_DIM = HEAD_DIM // 2
ROPE_BASE = 10000.0
EPS = 1e-6
Q_SCALE = HEAD_DIM ** -0.5 * 1.4426950408889634

LANES = 128
SUBLANES = 8
BF16_ROWS = 16
VMEM_LIMIT_BYTES = 56 * 1024 * 1024

ROW_TILE = 512
Q_TILE = 512
KV_TILE = 512
ATTN_ROW_BLOCK = 32
ATTN_PAIRS_PER_TRIP = 3
POOL_ROW_CHUNK = 128
LRU_TIME_TILE = 1024
SCAN_GROUPS = 4
LRU_LAYER_VARIANTS = ((SCAN_GROUPS, LRU_TIME_TILE), (8, LRU_TIME_TILE), (SCAN_GROUPS, 2048), (2, LRU_TIME_TILE))
INPROJ_LAYER_TILES = (ROW_TILE, 256, ROW_TILE, ROW_TILE)
OUTPROJ_LAYER_TILES = (ROW_TILE, 1024, 256, ROW_TILE)
SCAN_PITCH_PAD = 4
SCAN_MAX_UNROLL = 10


def _silu(x):
    h = 0.5 * x
    return h * jnp.tanh(h) + h


def _compiler_params(semantics):
    return pltpu.CompilerParams(dimension_semantics=semantics, vmem_limit_bytes=VMEM_LIMIT_BYTES)


def _inproj_kernel(x_ref, g_ref, w_ref, qn_ref, kn_ref, cos_ref, sin_ref,
                   zp_ref, zl_ref, q_ref, k_ref, v_ref, ga_ref, *, widths):
    pool2, lru2, qw, kw, vw, gw = widths
    x = x_ref[...]
    ms = jnp.mean(x * x, axis=-1, keepdims=True)
    h = (x * lax.rsqrt(ms + EPS) * g_ref[...]).astype(BF16)

    def proj(lo, width):
        return jnp.dot(h, w_ref[:, lo:lo + width], preferred_element_type=F32)

    cos = cos_ref[...]
    sin = sin_ref[...]
    lane = lax.broadcasted_iota(jnp.int32, cos.shape, 1)
    low_half = (lane & (ROPE_AXIS_DIM // 2)) == 0

    def norm_rope(zh, gain, scale):
        msh = jnp.mean(zh * zh, axis=-1, keepdims=True)
        y = zh * lax.rsqrt(msh + EPS) * gain
        partner = jnp.where(low_half,
                            pltpu.roll(y, HEAD_DIM - ROPE_AXIS_DIM // 2, 1),
                            pltpu.roll(y, ROPE_AXIS_DIM // 2, 1))
        return (y * cos + partner * sin) * scale

    q_off = pool2 + lru2
    zq = proj(q_off, qw)
    for hd in range(qw // HEAD_DIM):
        sl = slice(hd * HEAD_DIM, (hd + 1) * HEAD_DIM)
        q_ref[:, sl] = norm_rope(zq[:, sl], qn_ref[...], Q_SCALE).astype(BF16)
    zk = proj(q_off + qw, kw)
    for hd in range(kw // HEAD_DIM):
        sl = slice(hd * HEAD_DIM, (hd + 1) * HEAD_DIM)
        k_ref[:, sl] = norm_rope(zk[:, sl], kn_ref[...], 1.0).astype(BF16)
    v_ref[...] = proj(q_off + qw + kw, vw).astype(BF16)
    ga_ref[...] = proj(q_off + qw + kw + vw, gw).astype(BF16)
    zp_ref[...] = proj(0, pool2).astype(BF16)
    zl_ref[...] = proj(pool2, lru2).astype(BF16)


def _inproj(x2, norm_g, w_in, layer, q_norm, k_norm, cos_t, sin_t, seq_len, tm):
    n_rows, d_model = x2.shape
    d_in = w_in.shape[-1]
    d_mix = 2 * d_model
    pool_w, lru_w, attn_w = d_mix // 4, 3 * d_mix // 8, 3 * d_mix // 8
    kv_w = N_KV_HEADS * HEAD_DIM
    widths = (2 * pool_w, 2 * lru_w, attn_w, kv_w, kv_w, attn_w)
    assert sum(widths) == d_in
    assert n_rows % tm == 0 and seq_len % tm == 0
    tiles_per_seq = seq_len // tm

    row = lambda w: pl.BlockSpec((tm, w), lambda i: (i, 0))
    const = lambda shape: pl.BlockSpec(shape, lambda i: (0,) * len(shape))
    pos = pl.BlockSpec((tm, HEAD_DIM), lambda i: (i % tiles_per_seq, 0))
    out_shape = tuple(jax.ShapeDtypeStruct((n_rows, w), BF16) for w in widths)
    return pl.pallas_call(
        functools.partial(_inproj_kernel, widths=widths),
        grid=(n_rows // tm,),
        in_specs=[row(d_model), const((1, d_model)),
                  pl.BlockSpec((None, d_model, d_in), lambda i: (layer, 0, 0)),
                  const((1, HEAD_DIM)), const((1, HEAD_DIM)), pos, pos],
        out_specs=tuple(row(w) for w in widths),
        out_shape=out_shape,
        compiler_params=_compiler_params(("parallel",)),
        name="inproj",
    )(x2, norm_g.reshape(1, d_model), w_in, q_norm.reshape(1, HEAD_DIM),
      k_norm.reshape(1, HEAD_DIM), cos_t, sin_t)


def _lru_kernel(u_ref, g_ref, cw_ref, cb_ref, wr_ref, br_ref, wi_ref, bi_ref, lam_ref, y_ref,
                ubuf, a_f, b_f, a_b, b_b, h_f, h_b, *, seq_len, pitch, n_groups, tc, unroll):
    n_chunks = seq_len // tc
    pad_rows = n_groups * SUBLANES * pitch - seq_len
    halo = SUBLANES

    ubuf[0:halo, :] = jnp.zeros((halo, LANES), F32)
    ubuf[halo + seq_len:2 * halo + seq_len, :] = jnp.zeros((halo, LANES), F32)

    def stage(c, _):
        r0 = pl.multiple_of(c * tc, tc)
        ubuf[pl.ds(halo + r0, tc), :] = u_ref[0, pl.ds(r0, tc), :].astype(F32)
        return 0
    lax.fori_loop(0, n_chunks, stage, 0)

    for a_ref, b_ref in ((a_f, b_f), (a_b, b_b)):
        a_ref[seq_len:seq_len + pad_rows, :] = jnp.ones((pad_rows, LANES), F32)
        b_ref[seq_len:seq_len + pad_rows, :] = jnp.zeros((pad_rows, LANES), F32)

    lam = lam_ref[...]
    log_sig_lam = jnp.minimum(lam, 0.0) - jnp.log1p(jnp.exp(-jnp.abs(lam)))
    half_c = (0.5 * LRU_C) * log_sig_lam
    cw = cw_ref[...]
    cb = cb_ref[...]
    br_half = 0.5 * br_ref[...]
    bi_half = 0.5 * bi_ref[...]

    def gates(c, _):
        r0 = pl.multiple_of(c * tc, tc)
        xc = cb
        for k in range(CONV_WIDTH):
            xk = ubuf[pl.ds(halo + r0 + (k - CONV_PAD_LEFT), tc), :]
            xc = xc + xk * cw[k:k + 1, :]
        xcb = xc.astype(BF16)
        half_x = 0.5 * xc
        for d, (a_ref, b_ref) in enumerate(((a_f, b_f), (a_b, b_b))):
            tr = jnp.tanh(jnp.dot(xcb, wr_ref[d, 0], preferred_element_type=F32) + br_half[d:d + 1, :])
            ti = jnp.tanh(jnp.dot(xcb, wi_ref[d, 0], preferred_element_type=F32) + bi_half[d:d + 1, :])
            log_a = tr * half_c[d:d + 1, :] + half_c[d:d + 1, :]
            a = jnp.exp(log_a)
            sq = jnp.tanh(log_a) * (-1.0 - a * a)
            gain = jnp.where(sq > 0.0, sq * lax.rsqrt(sq), 0.0)
            a_ref[pl.ds(r0, tc), :] = a
            b_ref[pl.ds(r0, tc), :] = gain * ((ti + 1.0) * half_x)
        return 0
    lax.fori_loop(0, n_chunks, gates, 0)

    n_seq_chunks = n_groups * SUBLANES

    def strided(g, t):
        return pl.ds(g * SUBLANES * pitch + t, SUBLANES, stride=pitch)

    def sweep_totals(t, carry):
        tb = pitch - 1 - t
        out = []
        for g, (pf, hf, pb, hb) in enumerate(carry):
            af = a_f[strided(g, t), :]
            ab = a_b[strided(g, tb), :]
            out.append((pf * af, af * hf + b_f[strided(g, t), :],
                        pb * ab, ab * hb + b_b[strided(g, tb), :]))
        return tuple(out)

    ones = jnp.ones((SUBLANES, LANES), F32)
    zeros = jnp.zeros((SUBLANES, LANES), F32)
    totals = lax.fori_loop(0, pitch, sweep_totals, ((ones, zeros, ones, zeros),) * n_groups,
                           unroll=unroll)

    def chunk_row(field, k):
        s = k % SUBLANES
        return totals[k // SUBLANES][field][s:s + 1, :]
    cf = [jnp.zeros((1, LANES), F32)]
    for k in range(1, n_seq_chunks):
        cf.append(chunk_row(0, k - 1) * cf[-1] + chunk_row(1, k - 1))
    cbk = [jnp.zeros((1, LANES), F32)]
    for k in range(n_seq_chunks - 2, -1, -1):
        cbk.append(chunk_row(2, k + 1) * cbk[-1] + chunk_row(3, k + 1))
    cbk = cbk[::-1]
    entry = tuple((jnp.concatenate(cf[g * SUBLANES:(g + 1) * SUBLANES], axis=0),
                   jnp.concatenate(cbk[g * SUBLANES:(g + 1) * SUBLANES], axis=0))
                  for g in range(n_groups))

    def sweep_states(t, carry):
        tb = pitch - 1 - t
        out = []
        for g, (hf, hb) in enumerate(carry):
            hf = a_f[strided(g, t), :] * hf + b_f[strided(g, t), :]
            hb = a_b[strided(g, tb), :] * hb + b_b[strided(g, tb), :]
            h_f[strided(g, t), :] = hf
            h_b[strided(g, tb), :] = hb
            out.append((hf, hb))
        return tuple(out)
    lax.fori_loop(0, pitch, sweep_states, entry, unroll=unroll)

    def emit(c, _):
        r0 = pl.multiple_of(c * tc, tc)
        h = h_f[pl.ds(r0, tc), :] + h_b[pl.ds(r0, tc), :]
        g = g_ref[0, pl.ds(r0, tc), :].astype(F32)
        y_ref[0, pl.ds(r0, tc), :] = (h * _silu(g)).astype(BF16)
        return 0
    lax.fori_loop(0, n_chunks, emit, 0)


def _rg_lru(zl3, conv_w, conv_b, w_r, b_r, w_i, b_i, lam, layer, n_groups, tc):
    batch, seq_len, lru2 = zl3.shape
    lru_w = lru2 // 2
    n_blk = lru_w // LRU_BLOCK_DIM
    n_seq_chunks = n_groups * SUBLANES
    assert seq_len % tc == 0 and seq_len % n_seq_chunks == 0
    pitch = seq_len // n_seq_chunks + SCAN_PITCH_PAD
    rows = n_seq_chunks * pitch
    unroll = max(d for d in range(1, SCAN_MAX_UNROLL + 1) if pitch % d == 0)

    col = lambda r: pl.BlockSpec((r, LANES), lambda b, j: (0, j))
    seq = lambda off: pl.BlockSpec((1, seq_len, LANES), lambda b, j: (b, 0, j + off))
    wblk = pl.BlockSpec((None, 2, 1, LRU_BLOCK_DIM, LRU_BLOCK_DIM),
                        lambda b, j: (layer, 0, j, 0, 0))
    return pl.pallas_call(
        functools.partial(_lru_kernel, seq_len=seq_len, pitch=pitch, n_groups=n_groups,
                          tc=tc, unroll=unroll),
        grid=(batch, n_blk),
        in_specs=[seq(0), seq(n_blk), col(CONV_WIDTH), col(1), wblk, col(2), wblk, col(2), col(2)],
        out_specs=pl.BlockSpec((1, seq_len, LANES), lambda b, j: (b, 0, j)),
        out_shape=jax.ShapeDtypeStruct((batch, seq_len, lru_w), BF16),
        scratch_shapes=[pltpu.VMEM((seq_len + 2 * SUBLANES, LANES), F32)]
                       + [pltpu.VMEM((rows, LANES), F32)] * 6,
        compiler_params=_compiler_params(("parallel", "parallel")),
        name="rg_lru",
    )(zl3, zl3, conv_w, conv_b.reshape(1, lru_w), w_r, b_r, w_i, b_i, lam)


def _attn_kernel(q_ref, k_ref, v_ref, g_ref, o_ref, s_a, s_b, p_a, p_b, m_ref, l_ref, acc_ref,
                 *, tk, rb, pairs_per_trip):
    tq = q_ref.shape[1]
    seq_len = k_ref.shape[1]
    rows = GQA_GROUP * tq
    n_chunks = seq_len // tk
    q = q_ref[0]
    qs = jnp.concatenate([q[:, i * HEAD_DIM:(i + 1) * HEAD_DIM] for i in range(GQA_GROUP)], axis=0)

    def scores(c, s_ref):
        r0 = pl.multiple_of(c * tk, tk)
        kc = k_ref[0, pl.ds(r0, tk), :]
        s_ref[...] = lax.dot_general(qs, kc, (((1,), (1,)), ((), ())), preferred_element_type=F32)

    def softmax(s_ref, p_ref):
        for r in range(rows // rb):
            rs = slice(r * rb, (r + 1) * rb)
            s = [s_ref[rs, j * LANES:(j + 1) * LANES] for j in range(tk // LANES)]
            mx = functools.reduce(jnp.maximum, s)
            m_old = m_ref[rs, :]
            m_new = jnp.maximum(m_old, jnp.max(mx, axis=-1, keepdims=True))
            alpha = jnp.exp2(m_old - m_new)
            p = [jnp.exp2(sj - m_new) for sj in s]
            l_ref[rs, :] = alpha * l_ref[rs, :] + functools.reduce(jnp.add, p)
            for j, pj in enumerate(p):
                p_ref[rs, j * LANES:(j + 1) * LANES] = pj.astype(BF16)
            acc_ref[rs, :] = alpha * acc_ref[rs, :]
            m_ref[rs, :] = m_new

    def weighted_values(c, p_ref):
        r0 = pl.multiple_of(c * tk, tk)
        vc = v_ref[0, pl.ds(r0, tk), :]
        acc_ref[...] += jnp.dot(p_ref[...], vc, preferred_element_type=F32)

    m_ref[...] = jnp.full(m_ref.shape, -jnp.inf, F32)
    l_ref[...] = jnp.zeros(l_ref.shape, F32)
    acc_ref[...] = jnp.zeros(acc_ref.shape, F32)
    scores(0, s_a)
    scores(1, s_b)
    softmax(s_a, p_a)

    def pipeline_pair(c):
        weighted_values(c, p_a)
        scores(c + 2, s_a)
        softmax(s_b, p_b)
        weighted_values(c + 1, p_b)
        scores(c + 3, s_b)
        softmax(s_a, p_a)

    n_pairs = n_chunks // 2 - 1

    def body(i, _):
        for k in range(pairs_per_trip):
            pipeline_pair(2 * (pairs_per_trip * i + k))
        return 0
    lax.fori_loop(0, n_pairs // pairs_per_trip, body, 0)
    for pair in range(n_pairs - n_pairs % pairs_per_trip, n_pairs):
        pipeline_pair(2 * pair)

    weighted_values(n_chunks - 2, p_a)
    softmax(s_b, p_b)
    weighted_values(n_chunks - 1, p_b)

    o = acc_ref[...] / jnp.sum(l_ref[...], axis=-1, keepdims=True)
    o = jnp.concatenate([o[i * tq:(i + 1) * tq, :] for i in range(GQA_GROUP)], axis=1)
    o_ref[0] = (o * _silu(g_ref[0].astype(F32))).astype(BF16)


def _attention(q3, k3, v3, g3):
    batch, seq_len, attn_w = q3.shape
    tq, tk = Q_TILE, KV_TILE
    assert seq_len % tq == 0 and seq_len % (2 * tk) == 0 and seq_len >= 4 * tk
    gw = GQA_GROUP * HEAD_DIM
    rows = GQA_GROUP * tq
    qspec = pl.BlockSpec((1, tq, gw), lambda b, h, i: (b, i, h))
    kvspec = pl.BlockSpec((1, seq_len, HEAD_DIM), lambda b, h, i: (b, 0, h))
    return pl.pallas_call(
        functools.partial(_attn_kernel, tk=tk, rb=ATTN_ROW_BLOCK,
                          pairs_per_trip=ATTN_PAIRS_PER_TRIP),
        grid=(batch, N_KV_HEADS, seq_len // tq),
        in_specs=[qspec, kvspec, kvspec, qspec],
        out_specs=qspec,
        out_shape=jax.ShapeDtypeStruct((batch, seq_len, attn_w), BF16),
        scratch_shapes=[pltpu.VMEM((rows, tk), F32)] * 2 + [pltpu.VMEM((rows, tk), BF16)] * 2
                       + [pltpu.VMEM((rows, LANES), F32)] * 2
                       + [pltpu.VMEM((rows, HEAD_DIM), F32)],
        compiler_params=_compiler_params(("parallel", "parallel", "parallel")),
        name="attention",
    )(q3, k3, v3, g3)


def _outproj_kernel(x_ref, zp_ref, prev_ref, next_ref, yl_ref, ya_ref, pw_ref, ps_ref, wo_ref,
                    o_ref, ext_ref, yp_ref, y_ref, *, seq_len, pool_w):
    tm = x_ref.shape[0]
    tiles_per_seq = seq_len // tm
    t0 = (pl.program_id(0) % tiles_per_seq) * tm
    has_prev = (t0 > 0).astype(F32)
    has_next = (t0 + tm < seq_len).astype(F32)
    halo = BF16_ROWS

    lru_w = yl_ref.shape[1]
    y_ref[:, 0:lru_w] = yl_ref[...]
    y_ref[:, lru_w:] = ya_ref[...]
    acc = jnp.dot(y_ref[...], wo_ref[pool_w:, :], preferred_element_type=F32)

    ext_ref[0:halo, :] = prev_ref[...].astype(F32) * has_prev
    ext_ref[halo:halo + tm, :] = zp_ref[:, 0:pool_w].astype(F32)
    ext_ref[halo + tm:, :] = next_ref[...].astype(F32) * has_next

    rc = POOL_ROW_CHUNK
    for c in range(tm // rc):
        t = t0 + c * rc + lax.broadcasted_iota(jnp.int32, (rc, 1), 0)
        for g, w in enumerate(POOL_WINDOWS):
            half = w // 2
            sl = slice(g * LANES, (g + 1) * LANES)
            ext = ext_ref[c * rc:c * rc + rc + 2 * halo, sl]
            u = ext[halo:halo + rc, :]
            span = 1
            while span < w:
                ext = ext + pltpu.roll(ext, span, 0)
                span *= 2
            if half > 1:
                ext = pltpu.roll(ext, rc + 2 * halo - (half - 1), 0)
            win = ext[halo:halo + rc, :]
            cnt = jnp.minimum(t + half, seq_len) - jnp.maximum(t - half, 0)
            pooled = win * (1.0 / cnt.astype(F32)) - u
            mixed = jnp.dot(pooled.astype(BF16), pw_ref[g], preferred_element_type=F32)
            gate = zp_ref[c * rc:(c + 1) * rc, pool_w + g * LANES:pool_w + (g + 1) * LANES].astype(F32)
            yp_ref[c * rc:(c + 1) * rc, sl] = (mixed * ps_ref[:, sl] * _silu(gate)).astype(BF16)

    acc = acc + jnp.dot(yp_ref[...], wo_ref[0:pool_w, :], preferred_element_type=F32)
    o_ref[...] = x_ref[...] + acc


def _outproj(x2, zp, y_lru, y_attn, pool_wt, pool_scale, w_out, layer, seq_len, tm):
    n_rows, d_model = x2.shape
    d_mix = w_out.shape[1]
    pool_w = d_mix // 4
    lru_w = y_lru.shape[1]
    attn_w = y_attn.shape[1]
    halo_blocks = tm // BF16_ROWS
    last_halo = n_rows // BF16_ROWS - 1

    row = lambda w: pl.BlockSpec((tm, w), lambda i: (i, 0))
    const = lambda shape: pl.BlockSpec(shape, lambda i: (0,) * len(shape))
    prev = pl.BlockSpec((BF16_ROWS, pool_w), lambda i: (jnp.maximum(i * halo_blocks - 1, 0), 0))
    nxt = pl.BlockSpec((BF16_ROWS, pool_w),
                       lambda i: (jnp.minimum((i + 1) * halo_blocks, last_halo), 0))
    return pl.pallas_call(
        functools.partial(_outproj_kernel, seq_len=seq_len, pool_w=pool_w),
        grid=(n_rows // tm,),
        in_specs=[row(d_model), row(2 * pool_w), prev, nxt, row(lru_w), row(attn_w),
                  pl.BlockSpec((None,) + pool_wt.shape[1:], lambda i: (layer, 0, 0, 0)),
                  const((1, pool_w)),
                  pl.BlockSpec((None, d_mix, d_model), lambda i: (layer, 0, 0))],
        out_specs=row(d_model),
        out_shape=jax.ShapeDtypeStruct((n_rows, d_model), F32),
        scratch_shapes=[pltpu.VMEM((tm + 2 * BF16_ROWS, pool_w), F32),
                        pltpu.VMEM((tm, pool_w), BF16),
                        pltpu.VMEM((tm, d_mix - pool_w), BF16)],
        compiler_params=_compiler_params(("parallel",)),
        name="outproj",
    )(x2, zp, zp, zp, y_lru, y_attn, pool_wt, pool_scale.reshape(1, pool_w), w_out)


def _rope_tables(seq_len):
    t = np.arange(seq_len)
    inv = ROPE_BASE ** (-np.arange(0, ROPE_AXIS_DIM, 2, dtype=np.float64) / ROPE_AXIS_DIM)
    ang_r = (t // GRID_W)[:, None] * inv
    ang_c = (t % GRID_W)[:, None] * inv
    cos_t = np.concatenate([np.cos(ang_r)] * 2 + [np.cos(ang_c)] * 2, axis=1)
    sin_t = np.concatenate([-np.sin(ang_r), np.sin(ang_r), -np.sin(ang_c), np.sin(ang_c)], axis=1)
    return jnp.asarray(cos_t, F32), jnp.asarray(sin_t, F32)


def kernel(x, norm_g, w_in, pool_w, pool_scale, conv_w, conv_b, lru_wr, lru_br, lru_wi, lru_bi,
           lru_lam, q_norm, k_norm, w_out):
    batch, seq_len, d_model = x.shape
    depth = w_in.shape[0]
    n_rows = batch * seq_len
    cos_t, sin_t = _rope_tables(seq_len)
    w_in_b = w_in.astype(BF16)
    w_out_b = w_out.astype(BF16)
    pool_w_b = pool_w.astype(BF16)
    wr_b = (0.5 * lru_wr).astype(BF16)
    wi_b = (0.5 * lru_wi).astype(BF16)

    x2 = x.reshape(n_rows, d_model)
    for l in range(depth):
        zp, zl, q, k, v, ga = _inproj(x2, norm_g[l], w_in_b, l, q_norm[l], k_norm[l], cos_t, sin_t,
                                      seq_len, INPROJ_LAYER_TILES[l % 4])
        seq3 = lambda a: a.reshape(batch, seq_len, a.shape[-1])
        y_lru = _rg_lru(seq3(zl), conv_w[l], conv_b[l], wr_b, lru_br[l], wi_b, lru_bi[l],
                        lru_lam[l], l, *LRU_LAYER_VARIANTS[l % 4])
        y_attn = _attention(seq3(q), seq3(k), seq3(v), seq3(ga))
        x2 = _outproj(x2, zp, y_lru.reshape(n_rows, -1), y_attn.reshape(n_rows, -1),
                      pool_w_b, pool_scale[l], w_out_b, l, seq_len, OUTPROJ_LAYER_TILES[l % 4])
    return x2.reshape(batch, seq_len, d_model)
```

```python
import functools

import jax
import jax.numpy as jnp
import numpy as np
from jax import lax
from jax.experimental import pallas as pl
from jax.experimental.pallas import tpu as pltpu

F32 = jnp.float32
BF16 = jnp.bfloat16

GRID_W = 64
POOL_WINDOWS = (2, 4, 8, 16)
HEAD_DIM = 128
N_Q_HEADS = 6
N_KV_HEADS = 2
GQA_GROUP = N_Q_HEADS // N_KV_HEADS
LRU_BLOCK_DIM = 128
LRU_C = 8.0
CONV_WIDTH = 4
CONV_PAD_LEFT = 1
ROPE_AXIS_DIM = HEAD_DIM // 2
ROPE_BASE = 10000.0
EPS = 1e-6
Q_SCALE = HEAD_DIM ** -0.5 * 1.4426950408889634

LANES = 128
SUBLANES = 8
BF16_ROWS = 16
VMEM_LIMIT_BYTES = 56 * 1024 * 1024

INPROJ_ROW_TILE = 512
OUTPROJ_ROW_TILE = 1024
Q_TILE = 512
KV_TILE = 512
ATTN_ROW_BLOCK = 32
ATTN_PAIRS_PER_TRIP = 3
POOL_ROW_CHUNK = 128
LRU_TIME_TILE = 1024
SCAN_GROUPS = 4
SCAN_PITCH_PAD = 4
SCAN_MAX_UNROLL = 10


def _silu(x):
    h = 0.5 * x
    return h * jnp.tanh(h) + h


def _compiler_params(semantics):
    return pltpu.CompilerParams(dimension_semantics=semantics, vmem_limit_bytes=VMEM_LIMIT_BYTES)


def _inproj_kernel(x_ref, g_ref, w_ref, qn_ref, kn_ref, cos_ref, sin_ref,
                   zp_ref, zl_ref, q_ref, k_ref, v_ref, ga_ref, *, widths):
    pool2, lru2, qw, kw, vw, gw = widths
    x = x_ref[...]
    ms = jnp.mean(x * x, axis=-1, keepdims=True)
    h = (x * lax.rsqrt(ms + EPS) * g_ref[...]).astype(BF16)

    def proj(lo, width):
        return jnp.dot(h, w_ref[:, lo:lo + width], preferred_element_type=F32)

    cos = cos_ref[...]
    sin = sin_ref[...]
    lane = lax.broadcasted_iota(jnp.int32, cos.shape, 1)
    low_half = (lane & (ROPE_AXIS_DIM // 2)) == 0

    def norm_rope(zh, gain, scale):
        msh = jnp.mean(zh * zh, axis=-1, keepdims=True)
        y = zh * lax.rsqrt(msh + EPS) * gain
        partner = jnp.where(low_half,
                            pltpu.roll(y, HEAD_DIM - ROPE_AXIS_DIM // 2, 1),
                            pltpu.roll(y, ROPE_AXIS_DIM // 2, 1))
        return (y * cos + partner * sin) * scale

    q_off = pool2 + lru2
    zq = proj(q_off, qw)
    for hd in range(qw // HEAD_DIM):
        sl = slice(hd * HEAD_DIM, (hd + 1) * HEAD_DIM)
        q_ref[:, sl] = norm_rope(zq[:, sl], qn_ref[...], Q_SCALE).astype(BF16)
    zk = proj(q_off + qw, kw)
    for hd in range(kw // HEAD_DIM):
        sl = slice(hd * HEAD_DIM, (hd + 1) * HEAD_DIM)
        k_ref[:, sl] = norm_rope(zk[:, sl], kn_ref[...], 1.0).astype(BF16)
    v_ref[...] = proj(q_off + qw + kw, vw).astype(BF16)
    ga_ref[...] = proj(q_off + qw + kw + vw, gw).astype(BF16)
    zp_ref[...] = proj(0, pool2).astype(BF16)
    zl_ref[...] = proj(pool2, lru2).astype(BF16)


def _inproj(x2, norm_g, w_in, layer, q_norm, k_norm, cos_t, sin_t, seq_len):
    n_rows, d_model = x2.shape
    d_in = w_in.shape[-1]
    d_mix = 2 * d_model
    pool_w, lru_w, attn_w = d_mix // 4, 3 * d_mix // 8, 3 * d_mix // 8
    kv_w = N_KV_HEADS * HEAD_DIM
    widths = (2 * pool_w, 2 * lru_w, attn_w, kv_w, kv_w, attn_w)
    assert sum(widths) == d_in
    tm = INPROJ_ROW_TILE
    assert n_rows % tm == 0 and seq_len % tm == 0
    tiles_per_seq = seq_len // tm

    row = lambda w: pl.BlockSpec((tm, w), lambda i: (i, 0))
    const = lambda shape: pl.BlockSpec(shape, lambda i: (0,) * len(shape))
    pos = pl.BlockSpec((tm, HEAD_DIM), lambda i: (i % tiles_per_seq, 0))
    out_shape = tuple(jax.ShapeDtypeStruct((n_rows, w), BF16) for w in widths)
    return pl.pallas_call(
        functools.partial(_inproj_kernel, widths=widths),
        grid=(n_rows // tm,),
        in_specs=[row(d_model), const((1, d_model)),
                  pl.BlockSpec((None, d_model, d_in), lambda i: (layer, 0, 0)),
                  const((1, HEAD_DIM)), const((1, HEAD_DIM)), pos, pos],
        out_specs=tuple(row(w) for w in widths),
        out_shape=out_shape,
        compiler_params=_compiler_params(("parallel",)),
        name="inproj",
    )(x2, norm_g.reshape(1, d_model), w_in, q_norm.reshape(1, HEAD_DIM),
      k_norm.reshape(1, HEAD_DIM), cos_t, sin_t)


def _lru_kernel(u_ref, g_ref, cw_ref, cb_ref, wr_ref, br_ref, wi_ref, bi_ref, lam_ref, y_ref,
                ubuf, a_f, b_f, a_b, b_b, h_f, h_b, *, seq_len, pitch, n_groups, tc, unroll):
    n_chunks = seq_len // tc
    pad_rows = n_groups * SUBLANES * pitch - seq_len
    halo = SUBLANES

    ubuf[0:halo, :] = jnp.zeros((halo, LANES), F32)
    ubuf[halo + seq_len:2 * halo + seq_len, :] = jnp.zeros((halo, LANES), F32)

    def stage(c, _):
        r0 = pl.multiple_of(c * tc, tc)
        ubuf[pl.ds(halo + r0, tc), :] = u_ref[0, pl.ds(r0, tc), :].astype(F32)
        return 0
    lax.fori_loop(0, n_chunks, stage, 0)

    for a_ref, b_ref in ((a_f, b_f), (a_b, b_b)):
        a_ref[seq_len:seq_len + pad_rows, :] = jnp.ones((pad_rows, LANES), F32)
        b_ref[seq_len:seq_len + pad_rows, :] = jnp.zeros((pad_rows, LANES), F32)

    lam = lam_ref[...]
    log_sig_lam = jnp.minimum(lam, 0.0) - jnp.log1p(jnp.exp(-jnp.abs(lam)))
    half_c = (0.5 * LRU_C) * log_sig_lam
    cw = cw_ref[...]
    cb = cb_ref[...]
    br_half = 0.5 * br_ref[...]
    bi_half = 0.5 * bi_ref[...]

    def gates(c, _):
        r0 = pl.multiple_of(c * tc, tc)
        xc = cb
        for k in range(CONV_WIDTH):
            xk = ubuf[pl.ds(halo + r0 + (k - CONV_PAD_LEFT), tc), :]
            xc = xc + xk * cw[k:k + 1, :]
        xcb = xc.astype(BF16)
        half_x = 0.5 * xc
        for d, (a_ref, b_ref) in enumerate(((a_f, b_f), (a_b, b_b))):
            tr = jnp.tanh(jnp.dot(xcb, wr_ref[d, 0], preferred_element_type=F32) + br_half[d:d + 1, :])
            ti = jnp.tanh(jnp.dot(xcb, wi_ref[d, 0], preferred_element_type=F32) + bi_half[d:d + 1, :])
            log_a = tr * half_c[d:d + 1, :] + half_c[d:d + 1, :]
            a = jnp.exp(log_a)
            sq = jnp.tanh(log_a) * (-1.0 - a * a)
            gain = jnp.where(sq > 0.0, sq * lax.rsqrt(sq), 0.0)
            a_ref[pl.ds(r0, tc), :] = a
            b_ref[pl.ds(r0, tc), :] = gain * ((ti + 1.0) * half_x)
        return 0
    lax.fori_loop(0, n_chunks, gates, 0)

    n_seq_chunks = n_groups * SUBLANES

    def strided(g, t):
        return pl.ds(g * SUBLANES * pitch + t, SUBLANES, stride=pitch)

    def sweep_totals(t, carry):
        tb = pitch - 1 - t
        out = []
        for g, (pf, hf, pb, hb) in enumerate(carry):
            af = a_f[strided(g, t), :]
            ab = a_b[strided(g, tb), :]
            out.append((pf * af, af * hf + b_f[strided(g, t), :],
                        pb * ab, ab * hb + b_b[strided(g, tb), :]))
        return tuple(out)

    ones = jnp.ones((SUBLANES, LANES), F32)
    zeros = jnp.zeros((SUBLANES, LANES), F32)
    totals = lax.fori_loop(0, pitch, sweep_totals, ((ones, zeros, ones, zeros),) * n_groups,
                           unroll=unroll)

    def chunk_row(field, k):
        s = k % SUBLANES
        return totals[k // SUBLANES][field][s:s + 1, :]
    cf = [jnp.zeros((1, LANES), F32)]
    for k in range(1, n_seq_chunks):
        cf.append(chunk_row(0, k - 1) * cf[-1] + chunk_row(1, k - 1))
    cbk = [jnp.zeros((1, LANES), F32)]
    for k in range(n_seq_chunks - 2, -1, -1):
        cbk.append(chunk_row(2, k + 1) * cbk[-1] + chunk_row(3, k + 1))
    cbk = cbk[::-1]
    entry = tuple((jnp.concatenate(cf[g * SUBLANES:(g + 1) * SUBLANES], axis=0),
                   jnp.concatenate(cbk[g * SUBLANES:(g + 1) * SUBLANES], axis=0))
                  for g in range(n_groups))

    def sweep_states(t, carry):
        tb = pitch - 1 - t
        out = []
        for g, (hf, hb) in enumerate(carry):
            hf = a_f[strided(g, t), :] * hf + b_f[strided(g, t), :]
            hb = a_b[strided(g, tb), :] * hb + b_b[strided(g, tb), :]
            h_f[strided(g, t), :] = hf
            h_b[strided(g, tb), :] = hb
            out.append((hf, hb))
        return tuple(out)
    lax.fori_loop(0, pitch, sweep_states, entry, unroll=unroll)

    def emit(c, _):
        r0 = pl.multiple_of(c * tc, tc)
        h = h_f[pl.ds(r0, tc), :] + h_b[pl.ds(r0, tc), :]
        g = g_ref[0, pl.ds(r0, tc), :].astype(F32)
        y_ref[0, pl.ds(r0, tc), :] = (h * _silu(g)).astype(BF16)
        return 0
    lax.fori_loop(0, n_chunks, emit, 0)


def _rg_lru(zl3, conv_w, conv_b, w_r, b_r, w_i, b_i, lam, layer):
    batch, seq_len, lru2 = zl3.shape
    lru_w = lru2 // 2
    n_blk = lru_w // LRU_BLOCK_DIM
    n_groups, tc = SCAN_GROUPS, LRU_TIME_TILE
    n_seq_chunks = n_groups * SUBLANES
    assert seq_len % tc == 0 and seq_len % n_seq_chunks == 0
    pitch = seq_len // n_seq_chunks + SCAN_PITCH_PAD
    rows = n_seq_chunks * pitch
    unroll = max(d for d in range(1, SCAN_MAX_UNROLL + 1) if pitch % d == 0)

    col = lambda r: pl.BlockSpec((r, LANES), lambda b, j: (0, j))
    seq = lambda off: pl.BlockSpec((1, seq_len, LANES), lambda b, j: (b, 0, j + off))
    wblk = pl.BlockSpec((None, 2, 1, LRU_BLOCK_DIM, LRU_BLOCK_DIM),
                        lambda b, j: (layer, 0, j, 0, 0))
    return pl.pallas_call(
        functools.partial(_lru_kernel, seq_len=seq_len, pitch=pitch, n_groups=n_groups,
                          tc=tc, unroll=unroll),
        grid=(batch, n_blk),
        in_specs=[seq(0), seq(n_blk), col(CONV_WIDTH), col(1), wblk, col(2), wblk, col(2), col(2)],
        out_specs=pl.BlockSpec((1, seq_len, LANES), lambda b, j: (b, 0, j)),
        out_shape=jax.ShapeDtypeStruct((batch, seq_len, lru_w), BF16),
        scratch_shapes=[pltpu.VMEM((seq_len + 2 * SUBLANES, LANES), F32)]
                       + [pltpu.VMEM((rows, LANES), F32)] * 6,
        compiler_params=_compiler_params(("parallel", "parallel")),
        name="rg_lru",
    )(zl3, zl3, conv_w, conv_b.reshape(1, lru_w), w_r, b_r, w_i, b_i, lam)


def _attn_kernel(q_ref, k_ref, v_ref, g_ref, o_ref, s_a, s_b, p_a, p_b, m_ref, l_ref, acc_ref,
                 *, tk, rb, pairs_per_trip):
    tq = q_ref.shape[1]
    seq_len = k_ref.shape[1]
    rows = GQA_GROUP * tq
    n_chunks = seq_len // tk
    q = q_ref[0]
    qs = jnp.concatenate([q[:, i * HEAD_DIM:(i + 1) * HEAD_DIM] for i in range(GQA_GROUP)], axis=0)

    def scores(c, s_ref):
        r0 = pl.multiple_of(c * tk, tk)
        kc = k_ref[0, pl.ds(r0, tk), :]
        s_ref[...] = lax.dot_general(qs, kc, (((1,), (1,)), ((), ())), preferred_element_type=F32)

    def softmax(s_ref, p_ref):
        for r in range(rows // rb):
            rs = slice(r * rb, (r + 1) * rb)
            s = [s_ref[rs, j * LANES:(j + 1) * LANES] for j in range(tk // LANES)]
            mx = functools.reduce(jnp.maximum, s)
            m_old = m_ref[rs, :]
            m_new = jnp.maximum(m_old, jnp.max(mx, axis=-1, keepdims=True))
            alpha = jnp.exp2(m_old - m_new)
            p = [jnp.exp2(sj - m_new) for sj in s]
            l_ref[rs, :] = alpha * l_ref[rs, :] + functools.reduce(jnp.add, p)
            for j, pj in enumerate(p):
                p_ref[rs, j * LANES:(j + 1) * LANES] = pj.astype(BF16)
            acc_ref[rs, :] = alpha * acc_ref[rs, :]
            m_ref[rs, :] = m_new

    def weighted_values(c, p_ref):
        r0 = pl.multiple_of(c * tk, tk)
        vc = v_ref[0, pl.ds(r0, tk), :]
        acc_ref[...] += jnp.dot(p_ref[...], vc, preferred_element_type=F32)

    m_ref[...] = jnp.full(m_ref.shape, -jnp.inf, F32)
    l_ref[...] = jnp.zeros(l_ref.shape, F32)
    acc_ref[...] = jnp.zeros(acc_ref.shape, F32)
    scores(0, s_a)
    scores(1, s_b)
    softmax(s_a, p_a)

    def pipeline_pair(c):
        weighted_values(c, p_a)
        scores(c + 2, s_a)
        softmax(s_b, p_b)
        weighted_values(c + 1, p_b)
        scores(c + 3, s_b)
        softmax(s_a, p_a)

    n_pairs = n_chunks // 2 - 1

    def body(i, _):
        for k in range(pairs_per_trip):
            pipeline_pair(2 * (pairs_per_trip * i + k))
        return 0
    lax.fori_loop(0, n_pairs // pairs_per_trip, body, 0)
    for pair in range(n_pairs - n_pairs % pairs_per_trip, n_pairs):
        pipeline_pair(2 * pair)

    weighted_values(n_chunks - 2, p_a)
    softmax(s_b, p_b)
    weighted_values(n_chunks - 1, p_b)

    o = acc_ref[...] / jnp.sum(l_ref[...], axis=-1, keepdims=True)
    o = jnp.concatenate([o[i * tq:(i + 1) * tq, :] for i in range(GQA_GROUP)], axis=1)
    o_ref[0] = (o * _silu(g_ref[0].astype(F32))).astype(BF16)


def _attention(q3, k3, v3, g3):
    batch, seq_len, attn_w = q3.shape
    tq, tk = Q_TILE, KV_TILE
    assert seq_len % tq == 0 and seq_len % (2 * tk) == 0 and seq_len >= 4 * tk
    gw = GQA_GROUP * HEAD_DIM
    rows = GQA_GROUP * tq
    qspec = pl.BlockSpec((1, tq, gw), lambda b, h, i: (b, i, h))
    kvspec = pl.BlockSpec((1, seq_len, HEAD_DIM), lambda b, h, i: (b, 0, h))
    return pl.pallas_call(
        functools.partial(_attn_kernel, tk=tk, rb=ATTN_ROW_BLOCK,
                          pairs_per_trip=ATTN_PAIRS_PER_TRIP),
        grid=(batch, N_KV_HEADS, seq_len // tq),
        in_specs=[qspec, kvspec, kvspec, qspec],
        out_specs=qspec,
        out_shape=jax.ShapeDtypeStruct((batch, seq_len, attn_w), BF16),
        scratch_shapes=[pltpu.VMEM((rows, tk), F32)] * 2 + [pltpu.VMEM((rows, tk), BF16)] * 2
                       + [pltpu.VMEM((rows, LANES), F32)] * 2
                       + [pltpu.VMEM((rows, HEAD_DIM), F32)],
        compiler_params=_compiler_params(("parallel", "parallel", "parallel")),
        name="attention",
    )(q3, k3, v3, g3)


def _outproj_kernel(x_ref, zp_ref, prev_ref, next_ref, yl_ref, ya_ref, pw_ref, ps_ref, wo_ref,
                    o_ref, ext_ref, yp_ref, y_ref, *, seq_len, pool_w):
    tm = x_ref.shape[0]
    tiles_per_seq = seq_len // tm
    t0 = (pl.program_id(0) % tiles_per_seq) * tm
    has_prev = (t0 > 0).astype(F32)
    has_next = (t0 + tm < seq_len).astype(F32)
    halo = BF16_ROWS

    lru_w = yl_ref.shape[1]
    y_ref[:, 0:lru_w] = yl_ref[...]
    y_ref[:, lru_w:] = ya_ref[...]
    acc = jnp.dot(y_ref[...], wo_ref[pool_w:, :], preferred_element_type=F32)

    ext_ref[0:halo, :] = prev_ref[...].astype(F32) * has_prev
    ext_ref[halo:halo + tm, :] = zp_ref[:, 0:pool_w].astype(F32)
    ext_ref[halo + tm:, :] = next_ref[...].astype(F32) * has_next

    rc = POOL_ROW_CHUNK
    for c in range(tm // rc):
        t = t0 + c * rc + lax.broadcasted_iota(jnp.int32, (rc, 1), 0)
        for g, w in enumerate(POOL_WINDOWS):
            half = w // 2
            sl = slice(g * LANES, (g + 1) * LANES)
            ext = ext_ref[c * rc:c * rc + rc + 2 * halo, sl]
            u = ext[halo:halo + rc, :]
            span = 1
            while span < w:
                ext = ext + pltpu.roll(ext, span, 0)
                span *= 2
            if half > 1:
                ext = pltpu.roll(ext, rc + 2 * halo - (half - 1), 0)
            win = ext[halo:halo + rc, :]
            cnt = jnp.minimum(t + half, seq_len) - jnp.maximum(t - half, 0)
            pooled = win * (1.0 / cnt.astype(F32)) - u
            mixed = jnp.dot(pooled.astype(BF16), pw_ref[g], preferred_element_type=F32)
            gate = zp_ref[c * rc:(c + 1) * rc, pool_w + g * LANES:pool_w + (g + 1) * LANES].astype(F32)
            yp_ref[c * rc:(c + 1) * rc, sl] = (mixed * ps_ref[:, sl] * _silu(gate)).astype(BF16)

    acc = acc + jnp.dot(yp_ref[...], wo_ref[0:pool_w, :], preferred_element_type=F32)
    o_ref[...] = x_ref[...] + acc


def _outproj(x2, zp, y_lru, y_attn, pool_wt, pool_scale, w_out, layer, seq_len):
    n_rows, d_model = x2.shape
    d_mix = w_out.shape[1]
    pool_w = d_mix // 4
    lru_w = y_lru.shape[1]
    attn_w = y_attn.shape[1]
    tm = OUTPROJ_ROW_TILE
    assert n_rows % tm == 0 and seq_len % tm == 0
    halo_blocks = tm // BF16_ROWS
    last_halo = n_rows // BF16_ROWS - 1

    row = lambda w: pl.BlockSpec((tm, w), lambda i: (i, 0))
    const = lambda shape: pl.BlockSpec(shape, lambda i: (0,) * len(shape))
    prev = pl.BlockSpec((BF16_ROWS, pool_w), lambda i: (jnp.maximum(i * halo_blocks - 1, 0), 0))
    nxt = pl.BlockSpec((BF16_ROWS, pool_w),
                       lambda i: (jnp.minimum((i + 1) * halo_blocks, last_halo), 0))
    return pl.pallas_call(
        functools.partial(_outproj_kernel, seq_len=seq_len, pool_w=pool_w),
        grid=(n_rows // tm,),
        in_specs=[row(d_model), row(2 * pool_w), prev, nxt, row(lru_w), row(attn_w),
                  pl.BlockSpec((None,) + pool_wt.shape[1:], lambda i: (layer, 0, 0, 0)),
                  const((1, pool_w)),
                  pl.BlockSpec((None, d_mix, d_model), lambda i: (layer, 0, 0))],
        out_specs=row(d_model),
        out_shape=jax.ShapeDtypeStruct((n_rows, d_model), F32),
        scratch_shapes=[pltpu.VMEM((tm + 2 * BF16_ROWS, pool_w), F32),
                        pltpu.VMEM((tm, pool_w), BF16),
                        pltpu.VMEM((tm, d_mix - pool_w), BF16)],
        compiler_params=_compiler_params(("parallel",)),
        name="outproj",
    )(x2, zp, zp, zp, y_lru, y_attn, pool_wt, pool_scale.reshape(1, pool_w), w_out)


def _rope_tables(seq_len):
    t = np.arange(seq_len)
    inv = ROPE_BASE ** (-np.arange(0, ROPE_AXIS_DIM, 2, dtype=np.float64) / ROPE_AXIS_DIM)
    ang_r = (t // GRID_W)[:, None] * inv
    ang_c = (t % GRID_W)[:, None] * inv
    cos_t = np.concatenate([np.cos(ang_r)] * 2 + [np.cos(ang_c)] * 2, axis=1)
    sin_t = np.concatenate([-np.sin(ang_r), np.sin(ang_r), -np.sin(ang_c), np.sin(ang_c)], axis=1)
    return jnp.asarray(cos_t, F32), jnp.asarray(sin_t, F32)


def kernel(x, norm_g, w_in, pool_w, pool_scale, conv_w, conv_b, lru_wr, lru_br, lru_wi, lru_bi,
           lru_lam, q_norm, k_norm, w_out):
    batch, seq_len, d_model = x.shape
    depth = w_in.shape[0]
    n_rows = batch * seq_len
    cos_t, sin_t = _rope_tables(seq_len)
    w_in_b = w_in.astype(BF16)
    w_out_b = w_out.astype(BF16)
    pool_w_b = pool_w.astype(BF16)
    wr_b = (0.5 * lru_wr).astype(BF16)
    wi_b = (0.5 * lru_wi).astype(BF16)

    x2 = x.reshape(n_rows, d_model)
    for l in range(depth):
        zp, zl, q, k, v, ga = _inproj(x2, norm_g[l], w_in_b, l, q_norm[l], k_norm[l], cos_t, sin_t,
                                      seq_len)
        seq3 = lambda a: a.reshape(batch, seq_len, a.shape[-1])
        y_lru = _rg_lru(seq3(zl), conv_w[l], conv_b[l], wr_b, lru_br[l], wi_b, lru_bi[l],
                        lru_lam[l], l)
        y_attn = _attention(seq3(q), seq3(k), seq3(v), seq3(ga))
        x2 = _outproj(x2, zp, y_lru.reshape(n_rows, -1), y_attn.reshape(n_rows, -1),
                      pool_w_b, pool_scale[l], w_out_b, l, seq_len)
    return x2.reshape(batch, seq_len, d_model)
```

```python
import functools

import jax
import jax.numpy as jnp
import numpy as np
from jax import lax
from jax.experimental import pallas as pl
from jax.experimental.pallas import tpu as pltpu

F32 = jnp.float32
BF16 = jnp.bfloat16

GRID_W = 64
POOL_WINDOWS = (2, 4, 8, 16)
HEAD_DIM = 128
N_Q_HEADS = 6
N_KV_HEADS = 2
GQA_GROUP = N_Q_HEADS // N_KV_HEADS
LRU_BLOCK_DIM = 128
LRU_C = 8.0
CONV_WIDTH = 4
CONV_PAD_LEFT = 1
ROPE_AXIS_DIM = HEAD_DIM // 2
ROPE_BASE = 10000.0
EPS = 1e-6
Q_SCALE = HEAD_DIM ** -0.5 * 1.4426950408889634

LANES = 128
SUBLANES = 8
BF16_ROWS = 16
VMEM_LIMIT_BYTES = 56 * 1024 * 1024

INPROJ_ROW_TILE = 512
OUTPROJ_ROW_TILE = 1024
Q_TILE = 512
KV_TILE = 256
ATTN_ROW_BLOCK = 32
ATTN_PAIRS_PER_TRIP = 5
POOL_ROW_CHUNK = 128
LRU_TIME_TILE = 1024
SCAN_GROUPS = 4
SCAN_PITCH_PAD = 4
SCAN_MAX_UNROLL = 10


def _silu(x):
    h = 0.5 * x
    return h * jnp.tanh(h) + h


def _compiler_params(semantics):
    return pltpu.CompilerParams(dimension_semantics=semantics, vmem_limit_bytes=VMEM_LIMIT_BYTES)


def _inproj_kernel(x_ref, g_ref, w_ref, qn_ref, kn_ref, cos_ref, sin_ref,
                   zp_ref, zl_ref, q_ref, k_ref, v_ref, ga_ref, *, widths):
    pool2, lru2, qw, kw, vw, gw = widths
    x = x_ref[...]
    ms = jnp.mean(x * x, axis=-1, keepdims=True)
    h = (x * lax.rsqrt(ms + EPS) * g_ref[...]).astype(BF16)

    def proj(lo, width):
        return jnp.dot(h, w_ref[:, lo:lo + width], preferred_element_type=F32)

    cos = cos_ref[...]
    sin = sin_ref[...]
    lane = lax.broadcasted_iota(jnp.int32, cos.shape, 1)
    low_half = (lane & (ROPE_AXIS_DIM // 2)) == 0

    def norm_rope(zh, gain, scale):
        msh = jnp.mean(zh * zh, axis=-1, keepdims=True)
        y = zh * lax.rsqrt(msh + EPS) * gain
        partner = jnp.where(low_half,
                            pltpu.roll(y, HEAD_DIM - ROPE_AXIS_DIM // 2, 1),
                            pltpu.roll(y, ROPE_AXIS_DIM // 2, 1))
        return (y * cos + partner * sin) * scale

    q_off = pool2 + lru2
    zq = proj(q_off, qw)
    for hd in range(qw // HEAD_DIM):
        sl = slice(hd * HEAD_DIM, (hd + 1) * HEAD_DIM)
        q_ref[:, sl] = norm_rope(zq[:, sl], qn_ref[...], Q_SCALE).astype(BF16)
    zk = proj(q_off + qw, kw)
    for hd in range(kw // HEAD_DIM):
        sl = slice(hd * HEAD_DIM, (hd + 1) * HEAD_DIM)
        k_ref[:, sl] = norm_rope(zk[:, sl], kn_ref[...], 1.0).astype(BF16)
    v_ref[...] = proj(q_off + qw + kw, vw).astype(BF16)
    ga_ref[...] = proj(q_off + qw + kw + vw, gw).astype(BF16)
    zp_ref[...] = proj(0, pool2).astype(BF16)
    zl_ref[...] = proj(pool2, lru2).astype(BF16)


def _inproj(x2, norm_g, w_in, layer, q_norm, k_norm, cos_t, sin_t, seq_len):
    n_rows, d_model = x2.shape
    d_in = w_in.shape[-1]
    d_mix = 2 * d_model
    pool_w, lru_w, attn_w = d_mix // 4, 3 * d_mix // 8, 3 * d_mix // 8
    kv_w = N_KV_HEADS * HEAD_DIM
    widths = (2 * pool_w, 2 * lru_w, attn_w, kv_w, kv_w, attn_w)
    assert sum(widths) == d_in
    tm = INPROJ_ROW_TILE
    assert n_rows % tm == 0 and seq_len % tm == 0
    tiles_per_seq = seq_len // tm

    row = lambda w: pl.BlockSpec((tm, w), lambda i: (i, 0))
    const = lambda shape: pl.BlockSpec(shape, lambda i: (0,) * len(shape))
    pos = pl.BlockSpec((tm, HEAD_DIM), lambda i: (i % tiles_per_seq, 0))
    out_shape = tuple(jax.ShapeDtypeStruct((n_rows, w), BF16) for w in widths)
    return pl.pallas_call(
        functools.partial(_inproj_kernel, widths=widths),
        grid=(n_rows // tm,),
        in_specs=[row(d_model), const((1, d_model)),
                  pl.BlockSpec((None, d_model, d_in), lambda i: (layer, 0, 0)),
                  const((1, HEAD_DIM)), const((1, HEAD_DIM)), pos, pos],
        out_specs=tuple(row(w) for w in widths),
        out_shape=out_shape,
        compiler_params=_compiler_params(("parallel",)),
        name="inproj",
    )(x2, norm_g.reshape(1, d_model), w_in, q_norm.reshape(1, HEAD_DIM),
      k_norm.reshape(1, HEAD_DIM), cos_t, sin_t)


def _lru_kernel(u_ref, g_ref, cw_ref, cb_ref, wr_ref, br_ref, wi_ref, bi_ref, lam_ref, y_ref,
                ubuf, a_f, b_f, a_b, b_b, h_f, h_b, *, seq_len, pitch, n_groups, tc, unroll):
    n_chunks = seq_len // tc
    pad_rows = n_groups * SUBLANES * pitch - seq_len
    halo = SUBLANES

    ubuf[0:halo, :] = jnp.zeros((halo, LANES), F32)
    ubuf[halo + seq_len:2 * halo + seq_len, :] = jnp.zeros((halo, LANES), F32)

    def stage(c, _):
        r0 = pl.multiple_of(c * tc, tc)
        ubuf[pl.ds(halo + r0, tc), :] = u_ref[0, pl.ds(r0, tc), :].astype(F32)
        return 0
    lax.fori_loop(0, n_chunks, stage, 0)

    for a_ref, b_ref in ((a_f, b_f), (a_b, b_b)):
        a_ref[seq_len:seq_len + pad_rows, :] = jnp.ones((pad_rows, LANES), F32)
        b_ref[seq_len:seq_len + pad_rows, :] = jnp.zeros((pad_rows, LANES), F32)

    lam = lam_ref[...]
    log_sig_lam = jnp.minimum(lam, 0.0) - jnp.log1p(jnp.exp(-jnp.abs(lam)))
    half_c = (0.5 * LRU_C) * log_sig_lam
    cw = cw_ref[...]
    cb = cb_ref[...]
    br_half = 0.5 * br_ref[...]
    bi_half = 0.5 * bi_ref[...]

    def gates(c, _):
        r0 = pl.multiple_of(c * tc, tc)
        xc = cb
        for k in range(CONV_WIDTH):
            xk = ubuf[pl.ds(halo + r0 + (k - CONV_PAD_LEFT), tc), :]
            xc = xc + xk * cw[k:k + 1, :]
        xcb = xc.astype(BF16)
        half_x = 0.5 * xc
        for d, (a_ref, b_ref) in enumerate(((a_f, b_f), (a_b, b_b))):
            tr = jnp.tanh(jnp.dot(xcb, wr_ref[d, 0], preferred_element_type=F32) + br_half[d:d + 1, :])
            ti = jnp.tanh(jnp.dot(xcb, wi_ref[d, 0], preferred_element_type=F32) + bi_half[d:d + 1, :])
            log_a = tr * half_c[d:d + 1, :] + half_c[d:d + 1, :]
            a = jnp.exp(log_a)
            sq = jnp.tanh(log_a) * (-1.0 - a * a)
            gain = jnp.where(sq > 0.0, sq * lax.rsqrt(sq), 0.0)
            a_ref[pl.ds(r0, tc), :] = a
            b_ref[pl.ds(r0, tc), :] = gain * ((ti + 1.0) * half_x)
        return 0
    lax.fori_loop(0, n_chunks, gates, 0)

    n_seq_chunks = n_groups * SUBLANES

    def strided(g, t):
        return pl.ds(g * SUBLANES * pitch + t, SUBLANES, stride=pitch)

    def sweep_totals(t, carry):
        tb = pitch - 1 - t
        out = []
        for g, (pf, hf, pb, hb) in enumerate(carry):
            af = a_f[strided(g, t), :]
            ab = a_b[strided(g, tb), :]
            out.append((pf * af, af * hf + b_f[strided(g, t), :],
                        pb * ab, ab * hb + b_b[strided(g, tb), :]))
        return tuple(out)

    ones = jnp.ones((SUBLANES, LANES), F32)
    zeros = jnp.zeros((SUBLANES, LANES), F32)
    totals = lax.fori_loop(0, pitch, sweep_totals, ((ones, zeros, ones, zeros),) * n_groups,
                           unroll=unroll)

    def chunk_row(field, k):
        s = k % SUBLANES
        return totals[k // SUBLANES][field][s:s + 1, :]
    cf = [jnp.zeros((1, LANES), F32)]
    for k in range(1, n_seq_chunks):
        cf.append(chunk_row(0, k - 1) * cf[-1] + chunk_row(1, k - 1))
    cbk = [jnp.zeros((1, LANES), F32)]
    for k in range(n_seq_chunks - 2, -1, -1):
        cbk.append(chunk_row(2, k + 1) * cbk[-1] + chunk_row(3, k + 1))
    cbk = cbk[::-1]
    entry = tuple((jnp.concatenate(cf[g * SUBLANES:(g + 1) * SUBLANES], axis=0),
                   jnp.concatenate(cbk[g * SUBLANES:(g + 1) * SUBLANES], axis=0))
                  for g in range(n_groups))

    def sweep_states(t, carry):
        tb = pitch - 1 - t
        out = []
        for g, (hf, hb) in enumerate(carry):
            hf = a_f[strided(g, t), :] * hf + b_f[strided(g, t), :]
            hb = a_b[strided(g, tb), :] * hb + b_b[strided(g, tb), :]
            h_f[strided(g, t), :] = hf
            h_b[strided(g, tb), :] = hb
            out.append((hf, hb))
        return tuple(out)
    lax.fori_loop(0, pitch, sweep_states, entry, unroll=unroll)

    def emit(c, _):
        r0 = pl.multiple_of(c * tc, tc)
        h = h_f[pl.ds(r0, tc), :] + h_b[pl.ds(r0, tc), :]
        g = g_ref[0, pl.ds(r0, tc), :].astype(F32)
        y_ref[0, pl.ds(r0, tc), :] = (h * _silu(g)).astype(BF16)
        return 0
    lax.fori_loop(0, n_chunks, emit, 0)


def _rg_lru(zl3, conv_w, conv_b, w_r, b_r, w_i, b_i, lam, layer):
    batch, seq_len, lru2 = zl3.shape
    lru_w = lru2 // 2
    n_blk = lru_w // LRU_BLOCK_DIM
    n_groups, tc = SCAN_GROUPS, LRU_TIME_TILE
    n_seq_chunks = n_groups * SUBLANES
    assert seq_len % tc == 0 and seq_len % n_seq_chunks == 0
    pitch = seq_len // n_seq_chunks + SCAN_PITCH_PAD
    rows = n_seq_chunks * pitch
    unroll = max(d for d in range(1, SCAN_MAX_UNROLL + 1) if pitch % d == 0)

    col = lambda r: pl.BlockSpec((r, LANES), lambda b, j: (0, j))
    seq = lambda off: pl.BlockSpec((1, seq_len, LANES), lambda b, j: (b, 0, j + off))
    wblk = pl.BlockSpec((None, 2, 1, LRU_BLOCK_DIM, LRU_BLOCK_DIM),
                        lambda b, j: (layer, 0, j, 0, 0))
    return pl.pallas_call(
        functools.partial(_lru_kernel, seq_len=seq_len, pitch=pitch, n_groups=n_groups,
                          tc=tc, unroll=unroll),
        grid=(batch, n_blk),
        in_specs=[seq(0), seq(n_blk), col(CONV_WIDTH), col(1), wblk, col(2), wblk, col(2), col(2)],
        out_specs=pl.BlockSpec((1, seq_len, LANES), lambda b, j: (b, 0, j)),
        out_shape=jax.ShapeDtypeStruct((batch, seq_len, lru_w), BF16),
        scratch_shapes=[pltpu.VMEM((seq_len + 2 * SUBLANES, LANES), F32)]
                       + [pltpu.VMEM((rows, LANES), F32)] * 6,
        compiler_params=_compiler_params(("parallel", "parallel")),
        name="rg_lru",
    )(zl3, zl3, conv_w, conv_b.reshape(1, lru_w), w_r, b_r, w_i, b_i, lam)


def _attn_kernel(q_ref, k_ref, v_ref, g_ref, o_ref, s_a, s_b, p_a, p_b, m_ref, l_ref, acc_ref,
                 *, tk, rb, pairs_per_trip):
    tq = q_ref.shape[1]
    seq_len = k_ref.shape[1]
    rows = GQA_GROUP * tq
    n_chunks = seq_len // tk
    q = q_ref[0]
    qs = jnp.concatenate([q[:, i * HEAD_DIM:(i + 1) * HEAD_DIM] for i in range(GQA_GROUP)], axis=0)

    def scores(c, s_ref):
        r0 = pl.multiple_of(c * tk, tk)
        kc = k_ref[0, pl.ds(r0, tk), :]
        s_ref[...] = lax.dot_general(qs, kc, (((1,), (1,)), ((), ())), preferred_element_type=F32)

    def softmax(s_ref, p_ref):
        for r in range(rows // rb):
            rs = slice(r * rb, (r + 1) * rb)
            s = [s_ref[rs, j * LANES:(j + 1) * LANES] for j in range(tk // LANES)]
            mx = functools.reduce(jnp.maximum, s)
            m_old = m_ref[rs, :]
            m_new = jnp.maximum(m_old, jnp.max(mx, axis=-1, keepdims=True))
            alpha = jnp.exp2(m_old - m_new)
            p = [jnp.exp2(sj - m_new) for sj in s]
            l_ref[rs, :] = alpha * l_ref[rs, :] + functools.reduce(jnp.add, p)
            for j, pj in enumerate(p):
                p_ref[rs, j * LANES:(j + 1) * LANES] = pj.astype(BF16)
            acc_ref[rs, :] = alpha * acc_ref[rs, :]
            m_ref[rs, :] = m_new

    def weighted_values(c, p_ref):
        r0 = pl.multiple_of(c * tk, tk)
        vc = v_ref[0, pl.ds(r0, tk), :]
        acc_ref[...] += jnp.dot(p_ref[...], vc, preferred_element_type=F32)

    m_ref[...] = jnp.full(m_ref.shape, -jnp.inf, F32)
    l_ref[...] = jnp.zeros(l_ref.shape, F32)
    acc_ref[...] = jnp.zeros(acc_ref.shape, F32)
    scores(0, s_a)
    scores(1, s_b)
    softmax(s_a, p_a)

    def pipeline_pair(c):
        weighted_values(c, p_a)
        scores(c + 2, s_a)
        softmax(s_b, p_b)
        weighted_values(c + 1, p_b)
        scores(c + 3, s_b)
        softmax(s_a, p_a)

    n_pairs = n_chunks // 2 - 1

    def body(i, _):
        for k in range(pairs_per_trip):
            pipeline_pair(2 * (pairs_per_trip * i + k))
        return 0
    lax.fori_loop(0, n_pairs // pairs_per_trip, body, 0)
    for pair in range(n_pairs - n_pairs % pairs_per_trip, n_pairs):
        pipeline_pair(2 * pair)

    weighted_values(n_chunks - 2, p_a)
    softmax(s_b, p_b)
    weighted_values(n_chunks - 1, p_b)

    o = acc_ref[...] / jnp.sum(l_ref[...], axis=-1, keepdims=True)
    o = jnp.concatenate([o[i * tq:(i + 1) * tq, :] for i in range(GQA_GROUP)], axis=1)
    o_ref[0] = (o * _silu(g_ref[0].astype(F32))).astype(BF16)


def _attention(q3, k3, v3, g3):
    batch, seq_len, attn_w = q3.shape
    tq, tk = Q_TILE, KV_TILE
    assert seq_len % tq == 0 and seq_len % (2 * tk) == 0 and seq_len >= 4 * tk
    gw = GQA_GROUP * HEAD_DIM
    rows = GQA_GROUP * tq
    qspec = pl.BlockSpec((1, tq, gw), lambda b, h, i: (b, i, h))
    kvspec = pl.BlockSpec((1, seq_len, HEAD_DIM), lambda b, h, i: (b, 0, h))
    return pl.pallas_call(
        functools.partial(_attn_kernel, tk=tk, rb=ATTN_ROW_BLOCK,
                          pairs_per_trip=ATTN_PAIRS_PER_TRIP),
        grid=(batch, N_KV_HEADS, seq_len // tq),
        in_specs=[qspec, kvspec, kvspec, qspec],
        out_specs=qspec,
        out_shape=jax.ShapeDtypeStruct((batch, seq_len, attn_w), BF16),
        scratch_shapes=[pltpu.VMEM((rows, tk), F32)] * 2 + [pltpu.VMEM((rows, tk), BF16)] * 2
                       + [pltpu.VMEM((rows, LANES), F32)] * 2
                       + [pltpu.VMEM((rows, HEAD_DIM), F32)],
        compiler_params=_compiler_params(("parallel", "parallel", "parallel")),
        name="attention",
    )(q3, k3, v3, g3)


def _outproj_kernel(x_ref, zp_ref, prev_ref, next_ref, yl_ref, ya_ref, pw_ref, ps_ref, wo_ref,
                    o_ref, ext_ref, yp_ref, y_ref, *, seq_len, pool_w):
    tm = x_ref.shape[0]
    tiles_per_seq = seq_len // tm
    t0 = (pl.program_id(0) % tiles_per_seq) * tm
    has_prev = (t0 > 0).astype(F32)
    has_next = (t0 + tm < seq_len).astype(F32)
    halo = BF16_ROWS

    lru_w = yl_ref.shape[1]
    y_ref[:, 0:lru_w] = yl_ref[...]
    y_ref[:, lru_w:] = ya_ref[...]
    acc = jnp.dot(y_ref[...], wo_ref[pool_w:, :], preferred_element_type=F32)

    ext_ref[0:halo, :] = prev_ref[...].astype(F32) * has_prev
    ext_ref[halo:halo + tm, :] = zp_ref[:, 0:pool_w].astype(F32)
    ext_ref[halo + tm:, :] = next_ref[...].astype(F32) * has_next

    rc = POOL_ROW_CHUNK
    for c in range(tm // rc):
        t = t0 + c * rc + lax.broadcasted_iota(jnp.int32, (rc, 1), 0)
        for g, w in enumerate(POOL_WINDOWS):
            half = w // 2
            sl = slice(g * LANES, (g + 1) * LANES)
            ext = ext_ref[c * rc:c * rc + rc + 2 * halo, sl]
            u = ext[halo:halo + rc, :]
            span = 1
            while span < w:
                ext = ext + pltpu.roll(ext, span, 0)
                span *= 2
            if half > 1:
                ext = pltpu.roll(ext, rc + 2 * halo - (half - 1), 0)
            win = ext[halo:halo + rc, :]
            cnt = jnp.minimum(t + half, seq_len) - jnp.maximum(t - half, 0)
            pooled = win * (1.0 / cnt.astype(F32)) - u
            mixed = jnp.dot(pooled.astype(BF16), pw_ref[g], preferred_element_type=F32)
            gate = zp_ref[c * rc:(c + 1) * rc, pool_w + g * LANES:pool_w + (g + 1) * LANES].astype(F32)
            yp_ref[c * rc:(c + 1) * rc, sl] = (mixed * ps_ref[:, sl] * _silu(gate)).astype(BF16)

    acc = acc + jnp.dot(yp_ref[...], wo_ref[0:pool_w, :], preferred_element_type=F32)
    o_ref[...] = x_ref[...] + acc


def _outproj(x2, zp, y_lru, y_attn, pool_wt, pool_scale, w_out, layer, seq_len):
    n_rows, d_model = x2.shape
    d_mix = w_out.shape[1]
    pool_w = d_mix // 4
    lru_w = y_lru.shape[1]
    attn_w = y_attn.shape[1]
    tm = OUTPROJ_ROW_TILE
    assert n_rows % tm == 0 and seq_len % tm == 0
    halo_blocks = tm // BF16_ROWS
    last_halo = n_rows // BF16_ROWS - 1

    row = lambda w: pl.BlockSpec((tm, w), lambda i: (i, 0))
    const = lambda shape: pl.BlockSpec(shape, lambda i: (0,) * len(shape))
    prev = pl.BlockSpec((BF16_ROWS, pool_w), lambda i: (jnp.maximum(i * halo_blocks - 1, 0), 0))
    nxt = pl.BlockSpec((BF16_ROWS, pool_w),
                       lambda i: (jnp.minimum((i + 1) * halo_blocks, last_halo), 0))
    return pl.pallas_call(
        functools.partial(_outproj_kernel, seq_len=seq_len, pool_w=pool_w),
        grid=(n_rows // tm,),
        in_specs=[row(d_model), row(2 * pool_w), prev, nxt, row(lru_w), row(attn_w),
                  pl.BlockSpec((None,) + pool_wt.shape[1:], lambda i: (layer, 0, 0, 0)),
                  const((1, pool_w)),
                  pl.BlockSpec((None, d_mix, d_model), lambda i: (layer, 0, 0))],
        out_specs=row(d_model),
        out_shape=jax.ShapeDtypeStruct((n_rows, d_model), F32),
        scratch_shapes=[pltpu.VMEM((tm + 2 * BF16_ROWS, pool_w), F32),
                        pltpu.VMEM((tm, pool_w), BF16),
                        pltpu.VMEM((tm, d_mix - pool_w), BF16)],
        compiler_params=_compiler_params(("parallel",)),
        name="outproj",
    )(x2, zp, zp, zp, y_lru, y_attn, pool_wt, pool_scale.reshape(1, pool_w), w_out)


def _rope_tables(seq_len):
    t = np.arange(seq_len)
    inv = ROPE_BASE ** (-np.arange(0, ROPE_AXIS_DIM, 2, dtype=np.float64) / ROPE_AXIS_DIM)
    ang_r = (t // GRID_W)[:, None] * inv
    ang_c = (t % GRID_W)[:, None] * inv
    cos_t = np.concatenate([np.cos(ang_r)] * 2 + [np.cos(ang_c)] * 2, axis=1)
    sin_t = np.concatenate([-np.sin(ang_r), np.sin(ang_r), -np.sin(ang_c), np.sin(ang_c)], axis=1)
    return jnp.asarray(cos_t, F32), jnp.asarray(sin_t, F32)


def kernel(x, norm_g, w_in, pool_w, pool_scale, conv_w, conv_b, lru_wr, lru_br, lru_wi, lru_bi,
           lru_lam, q_norm, k_norm, w_out):
    batch, seq_len, d_model = x.shape
    depth = w_in.shape[0]
    n_rows = batch * seq_len
    cos_t, sin_t = _rope_tables(seq_len)
    w_in_b = w_in.astype(BF16)
    w_out_b = w_out.astype(BF16)
    pool_w_b = pool_w.astype(BF16)
    wr_b = (0.5 * lru_wr).astype(BF16)
    wi_b = (0.5 * lru_wi).astype(BF16)

    x2 = x.reshape(n_rows, d_model)
    for l in range(depth):
        zp, zl, q, k, v, ga = _inproj(x2, norm_g[l], w_in_b, l, q_norm[l], k_norm[l], cos_t, sin_t,
                                      seq_len)
        seq3 = lambda a: a.reshape(batch, seq_len, a.shape[-1])
        y_lru = _rg_lru(seq3(zl), conv_w[l], conv_b[l], wr_b, lru_br[l], wi_b, lru_bi[l],
                        lru_lam[l], l)
        y_attn = _attention(seq3(q), seq3(k), seq3(v), seq3(ga))
        x2 = _outproj(x2, zp, y_lru.reshape(n_rows, -1), y_attn.reshape(n_rows, -1),
                      pool_w_b, pool_scale[l], w_out_b, l, seq_len)
    return x2.reshape(batch, seq_len, d_model)
```
